```python
import math
import jax, jax.numpy as jnp
from jax import lax
import numpy as np

D_MODEL = 4096
BATCH = 2
SEQ = 4096
DEPTH = 2

N_MIXERS = 2
FFN_PERIOD = 2
N_ATTN_LAYERS = (DEPTH + N_MIXERS - 1) // N_MIXERS
N_CONV_LAYERS = DEPTH // N_MIXERS
N_DENSE_LAYERS = (DEPTH + FFN_PERIOD - 1) // FFN_PERIOD
N_MOE_LAYERS = DEPTH // FFN_PERIOD
DA_HEAD_DIM = 128
DA_HEADS = D_MODEL // (2 * DA_HEAD_DIM)
ROT_DIM = DA_HEAD_DIM // 4
ROPE_THETA = 500000.0
Q_BLOCK = 128
CONV_WIDTH = 31
D_FF = 14336
N_EXPERTS = 8
TOP_K = 2
D_FF_EXPERT = 4096
DEEPNORM_ALPHA = (2 * DEPTH) ** 0.25
DEEPNORM_BETA = (8 * DEPTH) ** -0.25
NORM_EPS = 1e-5

kernel_name = "hybrid_diffattn_conformer_moe_deepnorm_adaln"


def layer_norm(x, g, b):
    xf = x.astype(jnp.float32)
    mu = jnp.mean(xf, axis=-1, keepdims=True)
    xc = xf - mu
    var = jnp.mean(xc * xc, axis=-1, keepdims=True)
    y = xc * lax.rsqrt(var + NORM_EPS)
    return (y * g.astype(jnp.float32) + b.astype(jnp.float32)).astype(x.dtype)


def rms_norm(x, g):
    xf = x.astype(jnp.float32)
    y = xf * lax.rsqrt(jnp.mean(xf * xf, axis=-1, keepdims=True) + NORM_EPS)
    return (y * g.astype(jnp.float32)).astype(x.dtype)


def rope_tables(positions):
    inv_freq = 1.0 / (ROPE_THETA ** (jnp.arange(0, ROT_DIM, 2, dtype=jnp.float32) / ROT_DIM))
    ang = positions.astype(jnp.float32)[..., None] * inv_freq
    return jnp.cos(ang), jnp.sin(ang)


def apply_partial_rope(x, cos, sin):
    half = ROT_DIM // 2
    x1 = x[..., :half]
    x2 = x[..., half:ROT_DIM]
    rest = x[..., ROT_DIM:]
    c = cos[:, :, None, :].astype(x.dtype)
    s = sin[:, :, None, :].astype(x.dtype)
    return jnp.concatenate([x1 * c - x2 * s, x2 * c + x1 * s, rest], axis=-1)


def adaln(c, w, b):
    mod = (jax.nn.silu(c) @ w + b)[:, None, :]
    return jnp.split(mod, 3, axis=-1)


def deepnorm_residual(x, y, gate, g, b):
    return layer_norm(DEEPNORM_ALPHA * x + (1.0 + gate) * y, g, b)


def diff_attention(h, cos, sin, w_qkv, lam_vecs, subln_g, w_o, lambda_init):
    B, S, D = h.shape
    H, Dh = DA_HEADS, DA_HEAD_DIM
    q, k, v = jnp.split(h @ w_qkv, 3, axis=-1)
    q = apply_partial_rope(q.reshape(B, S, 2 * H, Dh), cos, sin).reshape(B, S, H, 2, Dh)
    k = apply_partial_rope(k.reshape(B, S, 2 * H, Dh), cos, sin).reshape(B, S, H, 2, Dh)
    v = v.reshape(B, S, H, 2 * Dh).astype(jnp.float32)
    lv = lam_vecs.astype(jnp.float32)
    lam = jnp.exp(jnp.sum(lv[0] * lv[1])) - jnp.exp(jnp.sum(lv[2] * lv[3])) + lambda_init
    nb = S // Q_BLOCK
    q_blocks = jnp.moveaxis(q.reshape(B, nb, Q_BLOCK, H, 2, Dh), 1, 0)
    k_pos = jnp.arange(S)
    scale = Dh ** -0.5

    def block(args):
        qb, bi = args
        s = jnp.einsum('bqhcd,bkhcd->bhcqk', qb, k, preferred_element_type=jnp.float32) * scale
        q_pos = bi * Q_BLOCK + jnp.arange(Q_BLOCK)
        causal = k_pos[None, :] <= q_pos[:, None]
        p = jax.nn.softmax(jnp.where(causal, s, -jnp.inf), axis=-1)
        a = p[:, :, 0] - lam * p[:, :, 1]
        return jnp.einsum('bhqk,bkhe->bqhe', a, v)

    o = lax.map(block, (q_blocks, jnp.arange(nb)))
    o = jnp.moveaxis(o, 0, 1).reshape(B, S, H, 2 * Dh).astype(h.dtype)
    o = rms_norm(o, subln_g) * (1.0 - lambda_init)
    return o.reshape(B, S, D) @ w_o


def conformer_conv(h, w_in, b_in, w_dw, b_dw, ln_g, ln_b, w_out, b_out):
    a, g = jnp.split(h @ w_in + b_in, 2, axis=-1)
    u = a * jax.nn.sigmoid(g)
    u = lax.conv_general_dilated(
        u, w_dw[:, None, :].astype(u.dtype), window_strides=(1,),
        padding=[(CONV_WIDTH - 1, 0)], dimension_numbers=('NWC', 'WIO', 'NWC'),
        feature_group_count=D_MODEL) + b_dw
    u = jax.nn.silu(layer_norm(u, ln_g, ln_b))
    return u @ w_out + b_out


def swiglu(h, w_in, w_out):
    g, u = jnp.split(h @ w_in, 2, axis=-1)
    return (jax.nn.silu(g) * u) @ w_out


def moe_swiglu(h, w_router, w_in, w_out):
    B, S, D = h.shape
    t = h.reshape(B * S, D)
    logits = (t @ w_router).astype(jnp.float32)
    top_v, top_i = lax.top_k(logits, TOP_K)
    top_w = jax.nn.softmax(top_v, axis=-1)
    gates = jnp.sum(jax.nn.one_hot(top_i, N_EXPERTS, dtype=jnp.float32) * top_w[..., None], axis=1)
    y = jnp.zeros_like(t)
    for e in range(N_EXPERTS):
        y = y + gates[:, e:e + 1].astype(t.dtype) * swiglu(t, w_in[e], w_out[e])
    return y.reshape(B, S, D)


def setup_inputs(seed: int = 0) -> dict:
    key = jax.random.key(seed)
    ks = jax.random.split(key, 32)
    D = D_MODEL
    f32 = jnp.float32

    def nrm(k, shape, s):
        return jax.random.normal(k, shape, f32) * s

    x = nrm(ks[0], (BATCH, SEQ, D), 1.0)
    c = nrm(ks[1], (BATCH, D), 1.0)
    positions = (jax.random.randint(ks[2], (BATCH, 1), 0, 1024, dtype=jnp.int32)
                 + jnp.arange(SEQ, dtype=jnp.int32)[None, :])
    ada_w = nrm(ks[3], (DEPTH, 2, D, 3 * D), 0.1 * D ** -0.5)
    ada_b = nrm(ks[4], (DEPTH, 2, 3 * D), 0.02)
    ln_g = 1.0 + nrm(ks[5], (DEPTH, 2, D), 0.02)
    ln_b = nrm(ks[6], (DEPTH, 2, D), 0.02)
    w_qk = nrm(ks[7], (N_ATTN_LAYERS, D, 2 * D), D ** -0.5)
    w_v = nrm(ks[8], (N_ATTN_LAYERS, D, D), DEEPNORM_BETA * D ** -0.5)
    attn_w_qkv = jnp.concatenate([w_qk, w_v], axis=-1)
    attn_lambda = nrm(ks[9], (N_ATTN_LAYERS, 4, DA_HEAD_DIM), 0.1)
    attn_subln_g = 1.0 + nrm(ks[10], (N_ATTN_LAYERS, 2 * DA_HEAD_DIM), 0.02)
    attn_w_o = nrm(ks[11], (N_ATTN_LAYERS, D, D), DEEPNORM_BETA * D ** -0.5)
    conv_w_in = nrm(ks[12], (N_CONV_LAYERS, D, 2 * D), D ** -0.5)
    conv_b_in = nrm(ks[13], (N_CONV_LAYERS, 2 * D), 0.02)
    conv_w_dw = nrm(ks[14], (N_CONV_LAYERS, CONV_WIDTH, D), CONV_WIDTH ** -0.5)
    conv_b_dw = nrm(ks[15], (N_CONV_LAYERS, D), 0.02)
    conv_ln_g = 1.0 + nrm(ks[16], (N_CONV_LAYERS, D), 0.02)
    conv_ln_b = nrm(ks[17], (N_CONV_LAYERS, D), 0.02)
    conv_w_out = nrm(ks[18], (N_CONV_LAYERS, D, D), DEEPNORM_BETA * D ** -0.5)
    conv_b_out = nrm(ks[19], (N_CONV_LAYERS, D), 0.02)
    ffn_w_in = nrm(ks[20], (N_DENSE_LAYERS, D, 2 * D_FF), D ** -0.5)
    ffn_w_out = nrm(ks[21], (N_DENSE_LAYERS, D_FF, D), DEEPNORM_BETA * D_FF ** -0.5)
    moe_w_router = nrm(ks[22], (N_MOE_LAYERS, D, N_EXPERTS), D ** -0.5)
    moe_w_in = nrm(ks[23], (N_MOE_LAYERS, N_EXPERTS, D, 2 * D_FF_EXPERT), D ** -0.5)
    moe_w_out = nrm(ks[24], (N_MOE_LAYERS, N_EXPERTS, D_FF_EXPERT, D), DEEPNORM_BETA * D_FF_EXPERT ** -0.5)
    return {
        "x": x, "c": c, "positions": positions,
        "ada_w": ada_w, "ada_b": ada_b, "ln_g": ln_g, "ln_b": ln_b,
        "attn_w_qkv": attn_w_qkv, "attn_lambda": attn_lambda,
        "attn_subln_g": attn_subln_g, "attn_w_o": attn_w_o,
        "conv_w_in": conv_w_in, "conv_b_in": conv_b_in, "conv_w_dw": conv_w_dw,
        "conv_b_dw": conv_b_dw, "conv_ln_g": conv_ln_g, "conv_ln_b": conv_ln_b,
        "conv_w_out": conv_w_out, "conv_b_out": conv_b_out,
        "ffn_w_in": ffn_w_in, "ffn_w_out": ffn_w_out,
        "moe_w_router": moe_w_router, "moe_w_in": moe_w_in, "moe_w_out": moe_w_out,
    }


def reference(x, c, positions, ada_w, ada_b, ln_g, ln_b,
              attn_w_qkv, attn_lambda, attn_subln_g, attn_w_o,
              conv_w_in, conv_b_in, conv_w_dw, conv_b_dw, conv_ln_g, conv_ln_b,
              conv_w_out, conv_b_out, ffn_w_in, ffn_w_out,
              moe_w_router, moe_w_in, moe_w_out):
    cos, sin = rope_tables(positions)
    for i in range(DEPTH):
        shift, scale, gate = adaln(c, ada_w[i, 0], ada_b[i, 0])
        h = x * (1.0 + scale) + shift
        jm = i // N_MIXERS
        if i % N_MIXERS == 0:
            lambda_init = 0.8 - 0.6 * math.exp(-0.3 * i)
            y = diff_attention(h, cos, sin, attn_w_qkv[jm], attn_lambda[jm],
                               attn_subln_g[jm], attn_w_o[jm], lambda_init)
        else:
            y = conformer_conv(h, conv_w_in[jm], conv_b_in[jm], conv_w_dw[jm], conv_b_dw[jm],
                               conv_ln_g[jm], conv_ln_b[jm], conv_w_out[jm], conv_b_out[jm])
        x = deepnorm_residual(x, y, gate, ln_g[i, 0], ln_b[i, 0])
        shift, scale, gate = adaln(c, ada_w[i, 1], ada_b[i, 1])
        h = x * (1.0 + scale) + shift
        jf = i // FFN_PERIOD
        if i % FFN_PERIOD == 0:
            y = swiglu(h, ffn_w_in[jf], ffn_w_out[jf])
        else:
            y = moe_swiglu(h, moe_w_router[jf], moe_w_in[jf], moe_w_out[jf])
        x = deepnorm_residual(x, y, gate, ln_g[i, 1], ln_b[i, 1])
    return x
```

```python
import functools
import math

import jax
import jax.numpy as jnp
from jax import lax
from jax.experimental import pallas as pl
from jax.experimental.pallas import tpu as pltpu

DA_HEAD_DIM = 128
ROPE_THETA = 500000.0
CONV_WIDTH = 31
N_EXPERTS = 8
TOP_K = 2
NORM_EPS = 1e-5
N_MIXERS = 2
FFN_PERIOD = 2

LANES = 128
V7X_VMEM_BYTES = 64 * 1024 * 1024
VMEM_REQUEST_CAP = V7X_VMEM_BYTES - 6 * 1024 * 1024
NEG_BIG = -1e30

f32 = jnp.float32
bf16 = jnp.bfloat16


def _params(n_grid, vmem_bytes):
    limit = min(int(vmem_bytes * 1.2) + (2 << 20), VMEM_REQUEST_CAP)
    return pltpu.CompilerParams(
        dimension_semantics=("arbitrary",) * n_grid, vmem_limit_bytes=limit)


def _nbytes(shape, dtype):
    return math.prod(shape) * jnp.dtype(dtype).itemsize


def _mm_body(*refs, n_pref, n_w, n_b, n_x, n_k, epilogue):
    pref = refs[:n_pref]
    x_ref = refs[n_pref]
    w_refs = refs[n_pref + 1:n_pref + 1 + n_w]
    b_refs = refs[n_pref + 1 + n_w:n_pref + 1 + n_w + n_b]
    e_refs = refs[n_pref + 1 + n_w + n_b:n_pref + 1 + n_w + n_b + n_x]
    o_ref = refs[n_pref + 1 + n_w + n_b + n_x]
    scratch = refs[n_pref + 2 + n_w + n_b + n_x:]
    wbf_ref = scratch[0]
    acc_ref = scratch[1] if n_k > 1 else None
    j, kk, i = pl.program_id(0), pl.program_id(1), pl.program_id(2)
    tm = x_ref.shape[0]

    if n_pref:
        te_ref, nu_ref = pref
        fresh = (i == 0) | (te_ref[i] != te_ref[jnp.maximum(i - 1, 0)])
        valid = i < nu_ref[0]
    else:
        fresh = i == 0
        valid = None

    @pl.when(fresh)
    def _():
        for n in range(n_w):
            wbf_ref[n] = w_refs[n][...].astype(bf16)

    def compute():
        x = x_ref[...]
        accs = [jnp.dot(x, wbf_ref[n], preferred_element_type=f32) for n in range(n_w)]
        if n_k == 1:
            epilogue(accs, b_refs, e_refs, o_ref, j)
            return
        rows = pl.ds(pl.multiple_of(i * tm, tm), tm)

        @pl.when(kk == 0)
        def _():
            for n in range(n_w):
                acc_ref[n, rows, :] = accs[n]

        @pl.when((kk > 0) & (kk < n_k - 1))
        def _():
            for n in range(n_w):
                acc_ref[n, rows, :] += accs[n]

        @pl.when(kk == n_k - 1)
        def _():
            epilogue([acc_ref[n, rows, :] + accs[n] for n in range(n_w)],
                     b_refs, e_refs, o_ref, j)

    if valid is None:
        compute()
    else:
        pl.when(valid)(compute)

        @pl.when(jnp.logical_not(valid))
        def _():
            o_ref[...] = jnp.zeros(o_ref.shape, o_ref.dtype)


def _linear(x, weights, epilogue, out_cols, out_dtype, *, tm, tn, n_k=1, biases=(),
            extras=(), group=None, name):
    M, K = x.shape
    tk = K // n_k
    assert M % tm == 0 and K % n_k == 0 and out_cols % tn == 0
    n_w, n_b = len(weights), len(biases)
    n_pref = 2 if group is not None else 0
    grid = (out_cols // tn, n_k, M // tm)
    last_k = n_k - 1

    if group is None:
        def row(i, *pref):
            return i
    else:
        def row(i, te, nu):
            return jnp.minimum(i, nu[0] - 1)

    in_specs = [pl.BlockSpec((tm, tk), lambda j, kk, i, *p: (row(i, *p), kk))]
    args = [x]
    for w, off in weights:
        if group is None:
            in_specs.append(pl.BlockSpec(
                (tk, tn), lambda j, kk, i, *p, off=off: (kk, off + j)))
        else:
            in_specs.append(pl.BlockSpec(
                (None, tk, tn), lambda j, kk, i, te, nu, off=off: (te[i], kk, off + j)))
        args.append(w)
    for b, off in biases:
        in_specs.append(pl.BlockSpec((1, tn), lambda j, kk, i, *p, off=off: (0, off + j)))
        args.append(b)
    for arr, blk, imap in extras:
        in_specs.append(pl.BlockSpec(blk, imap))
        args.append(arr)

    if n_k == 1:
        out_map = lambda j, kk, i, *p: (i, j)
    else:
        assert group is None
        out_map = lambda j, kk, i: (jnp.where(kk == last_k, i, 0), j)
    out_spec = pl.BlockSpec((tm, tn), out_map)

    scratch = [pltpu.VMEM((n_w, tk, tn), bf16)]
    vmem = (2 * _nbytes((tm, tk), bf16) + 2 * n_w * _nbytes((tk, tn), f32)
            + n_w * _nbytes((tk, tn), bf16) + 2 * _nbytes((tm, tn), out_dtype)
            + 3 * n_w * _nbytes((tm, tn), f32))
    for arr, blk, _ in extras:
        vmem += 2 * _nbytes([d for d in blk if d is not None], arr.dtype)
    if n_k > 1:
        scratch.append(pltpu.VMEM((n_w, M, tn), f32))
        vmem += n_w * _nbytes((M, tn), f32)

    body = functools.partial(_mm_body, n_pref=n_pref, n_w=n_w, n_b=n_b, n_x=len(extras),
                             n_k=n_k, epilogue=epilogue)
    call = pl.pallas_call(
        body,
        out_shape=jax.ShapeDtypeStruct((M, out_cols), out_dtype),
        grid_spec=pltpu.PrefetchScalarGridSpec(
            num_scalar_prefetch=n_pref, grid=grid, in_specs=in_specs, out_specs=out_spec,
            scratch_shapes=scratch),
        compiler_params=_params(3, vmem),
        name=name,
    )
    pref_args = list(group) if group is not None else []
    return call(*pref_args, *args)


def _epi_plain(accs, b_refs, e_refs, o_ref, j):
    y = accs[0]
    if b_refs:
        y = y + b_refs[0][...]
    o_ref[...] = y.astype(o_ref.dtype)


def _epi_swiglu(accs, b_refs, e_refs, o_ref, j):
    g, u = accs
    o_ref[...] = (g * jax.nn.sigmoid(g) * u).astype(o_ref.dtype)


def _epi_glu(accs, b_refs, e_refs, o_ref, j):
    a = accs[0] + b_refs[0][...]
    g = accs[1] + b_refs[1][...]
    o_ref[...] = (a * jax.nn.sigmoid(g)).astype(o_ref.dtype)


def _epi_rope(accs, b_refs, e_refs, o_ref, j, *, n_rope_blocks, rot_half):
    y = accs[0]
    cos_ref, sin_up_ref, sin_dn_ref = e_refs

    @pl.when(j < n_rope_blocks)
    def _():
        c, su, sd = cos_ref[...], sin_up_ref[...], sin_dn_ref[...]
        for g in range(y.shape[1] // LANES):
            yg = y[:, g * LANES:(g + 1) * LANES]
            up = pltpu.roll(yg, rot_half, axis=1)
            dn = pltpu.roll(yg, LANES - rot_half, axis=1)
            o_ref[:, g * LANES:(g + 1) * LANES] = (yg * c + up * su + dn * sd).astype(o_ref.dtype)

    @pl.when(j >= n_rope_blocks)
    def _():
        o_ref[...] = y.astype(o_ref.dtype)


def _adaln_body(c_ref, w_ref, b_ref, o_ref):
    c = c_ref[...]
    sc = (c * jax.nn.sigmoid(c)).astype(bf16)
    y = jnp.dot(sc, w_ref[...].astype(bf16), preferred_element_type=f32)
    o_ref[...] = y + b_ref[...]


def _adaln(c, ada_w, ada_b, *, tn=512):
    B, D = c.shape
    L = ada_w.shape[0] * ada_w.shape[1]
    w = ada_w.reshape(L, D, 3 * D)
    b = ada_b.reshape(L, 1, 3 * D)
    rows = 8
    c_pad = jnp.zeros((rows, D), f32).at[:B].set(c)
    vmem = 2 * _nbytes((D, tn), f32) + _nbytes((D, tn), bf16) + 4 * _nbytes((rows, D), f32)
    out = pl.pallas_call(
        _adaln_body,
        out_shape=jax.ShapeDtypeStruct((L, rows, 3 * D), f32),
        grid=(L, 3 * D // tn),
        in_specs=[pl.BlockSpec((rows, D), lambda m, j: (0, 0)),
                  pl.BlockSpec((None, D, tn), lambda m, j: (m, 0, j)),
                  pl.BlockSpec((None, 1, tn), lambda m, j: (m, 0, j))],
        out_specs=pl.BlockSpec((None, rows, tn), lambda m, j: (m, 0, j)),
        compiler_params=_params(2, vmem),
        name="adaln",
    )(c_pad, w, b)
    return out[:, :B]


def _modulate_body(x_ref, scale_ref, shift_ref, h_ref):
    h_ref[...] = (x_ref[...] * (1.0 + scale_ref[...]) + shift_ref[...]).astype(h_ref.dtype)


def _modulate(x2, scale, shift, seq, *, tm=512):
    T, D = x2.shape
    per_b = seq // tm
    vec = pl.BlockSpec((None, 1, D), lambda i: (i // per_b, 0, 0))
    return pl.pallas_call(
        _modulate_body,
        out_shape=jax.ShapeDtypeStruct((T, D), bf16),
        grid=(T // tm,),
        in_specs=[pl.BlockSpec((tm, D), lambda i: (i, 0)), vec, vec],
        out_specs=pl.BlockSpec((tm, D), lambda i: (i, 0)),
        compiler_params=_params(1, 2 * _nbytes((tm, D), f32) * 2),
        name="modulate",
    )(x2, scale, shift)


def _layer_norm_rows(z, g, b):
    mu = jnp.mean(z, axis=-1, keepdims=True)
    zc = z - mu
    var = jnp.mean(zc * zc, axis=-1, keepdims=True)
    return zc * lax.rsqrt(var + NORM_EPS) * g + b


def _top2_route(logits, n_experts):
    lane_i = lax.broadcasted_iota(jnp.int32, logits.shape, 1)
    lane = lane_i.astype(f32)
    lg = jnp.where(lane_i < n_experts, logits, -jnp.inf)
    m1 = jnp.max(lg, axis=1, keepdims=True)
    i1 = jnp.min(jnp.where(lg == m1, lane, float(LANES)), axis=1, keepdims=True)
    lg2 = jnp.where(lane == i1, -jnp.inf, lg)
    m2 = jnp.max(lg2, axis=1, keepdims=True)
    i2 = jnp.min(jnp.where(lg2 == m2, lane, float(LANES)), axis=1, keepdims=True)
    e = jnp.exp(m2 - m1)
    w1 = 1.0 / (1.0 + e)
    w2 = e / (1.0 + e)
    idx = jnp.where(lane_i == 0, i1, jnp.where(lane_i == 1, i2, 0.0)).astype(jnp.int32)
    wts = jnp.where(lane_i == 0, w1, jnp.where(lane_i == 1, w2, 0.0))
    return idx, wts


def _deepnorm_body(*refs, alpha, mode):
    x_ref, y_ref, gate_ref, g_ref, b_ref = refs[:5]
    z = alpha * x_ref[...] + (1.0 + gate_ref[...]) * y_ref[...]
    xo = _layer_norm_rows(z, g_ref[...], b_ref[...])
    if mode == "last":
        refs[5][...] = xo
        return
    scale_ref, shift_ref = refs[5:7]
    h = xo * (1.0 + scale_ref[...]) + shift_ref[...]
    if mode == "next":
        xo_ref, h_ref = refs[7:9]
        xo_ref[...] = xo
        h_ref[...] = h.astype(h_ref.dtype)
        return
    wr_ref, xo_ref, h_ref, idx_ref, wts_ref = refs[7:12]
    xo_ref[...] = xo
    h_ref[...] = h
    logits = jnp.dot(h, wr_ref[...], preferred_element_type=f32,
                     precision=lax.Precision.HIGHEST)
    idx, wts = _top2_route(logits, N_EXPERTS)
    idx_ref[...] = idx
    wts_ref[...] = wts


def _deepnorm(x2, y2, gate, g, b, seq, alpha, *, nxt=None, router=None, tm=256):
    T, D = x2.shape
    per_b = seq // tm
    rowblk = pl.BlockSpec((tm, D), lambda i: (i, 0))
    vec = pl.BlockSpec((None, 1, D), lambda i: (i // per_b, 0, 0))
    par = pl.BlockSpec((1, D), lambda i: (0, 0))
    lane_blk = pl.BlockSpec((tm, LANES), lambda i: (i, 0))
    in_specs = [rowblk, rowblk, vec, par, par]
    args = [x2, y2, gate, g.reshape(1, D), b.reshape(1, D)]
    out_shape = [jax.ShapeDtypeStruct((T, D), f32)]
    out_specs = [rowblk]
    n_blk = 3
    mode = "last"
    if nxt is not None:
        mode = "next"
        in_specs += [vec, vec]
        args += list(nxt)
        out_shape.append(jax.ShapeDtypeStruct((T, D), bf16 if router is None else f32))
        out_specs.append(rowblk)
        n_blk = 4
    if router is not None:
        mode = "router"
        in_specs.append(pl.BlockSpec((D, LANES), lambda i: (0, 0)))
        args.append(router)
        out_shape += [jax.ShapeDtypeStruct((T, LANES), jnp.int32),
                      jax.ShapeDtypeStruct((T, LANES), f32)]
        out_specs += [lane_blk, lane_blk]
    vmem = (2 * n_blk + 3) * _nbytes((tm, D), f32) + 2 * _nbytes((D, LANES), f32)
    return pl.pallas_call(
        functools.partial(_deepnorm_body, alpha=alpha, mode=mode),
        out_shape=out_shape,
        grid=(T // tm,),
        in_specs=in_specs,
        out_specs=out_specs,
        compiler_params=_params(1, vmem),
        name="deepnorm_" + mode,
    )(*args)


def _rope_table_body(pos_ref, invf_ref, cos_ref, sin_up_ref, sin_dn_ref, *, rot_half):
    ang = pos_ref[...] * invf_ref[...]
    lane = lax.broadcasted_iota(jnp.int32, ang.shape, 1)
    c, s = jnp.cos(ang), jnp.sin(ang)
    cos_ref[...] = jnp.where(lane < 2 * rot_half, c, 1.0)
    sin_up_ref[...] = jnp.where((lane >= rot_half) & (lane < 2 * rot_half), s, 0.0)
    sin_dn_ref[...] = jnp.where(lane < rot_half, -s, 0.0)


def _rope_tables(positions, rot_dim, *, tm=1024):
    T = positions.size
    half = rot_dim // 2
    inv_freq = 1.0 / (ROPE_THETA ** (jnp.arange(0, rot_dim, 2, dtype=f32) / rot_dim))
    invf = jnp.zeros((1, LANES), f32).at[0, :rot_dim].set(jnp.tile(inv_freq, 2))
    pos = positions.astype(f32).reshape(T, 1)
    tab = pl.BlockSpec((tm, LANES), lambda i: (i, 0))
    return pl.pallas_call(
        functools.partial(_rope_table_body, rot_half=half),
        out_shape=[jax.ShapeDtypeStruct((T, LANES), f32)] * 3,
        grid=(T // tm,),
        in_specs=[pl.BlockSpec((tm, 1), lambda i: (i, 0)),
                  pl.BlockSpec((1, LANES), lambda i: (0, 0))],
        out_specs=[tab, tab, tab],
        compiler_params=_params(1, 12 * _nbytes((tm, LANES), f32)),
        name="rope_tables",
    )(pos, invf)


def _diff_attn_body(q_ref, k_ref, v_ref, lam_ref, g_ref, o_ref, acc_ref, m_ref, l_ref,
                    *, t, scale, lambda_init):
    qi = pl.program_id(2)
    dh = DA_HEAD_DIM
    m_ref[...] = jnp.full(m_ref.shape, NEG_BIG, f32)
    l_ref[...] = jnp.zeros(l_ref.shape, f32)
    acc_ref[...] = jnp.zeros(acc_ref.shape, f32)

    def step(ki, masked):
        rows = pl.ds(pl.multiple_of(ki * t, t), t)
        v = v_ref[rows, :]
        for c in range(2):
            s = lax.dot_general(q_ref[:, c * dh:(c + 1) * dh], k_ref[rows, c * dh:(c + 1) * dh],
                                (((1,), (1,)), ((), ())), preferred_element_type=f32) * scale
            if masked:
                r = lax.broadcasted_iota(jnp.int32, s.shape, 0)
                col = lax.broadcasted_iota(jnp.int32, s.shape, 1)
                s = jnp.where(col <= r, s, NEG_BIG)
            m_old = m_ref[c]
            m_new = jnp.maximum(m_old, jnp.max(s, axis=1, keepdims=True))
            a = jnp.exp(m_old - m_new)
            p = jnp.exp(s - m_new)
            l_ref[c] = a * l_ref[c] + jnp.sum(p, axis=1, keepdims=True)
            acc_ref[c] = a * acc_ref[c] + jnp.dot(p.astype(bf16), v, preferred_element_type=f32)
            m_ref[c] = m_new

    def off_diag(ki, carry):
        step(ki, False)
        return carry

    lax.fori_loop(0, qi, off_diag, 0)
    step(qi, True)

    lv = lam_ref[...]
    lam = (jnp.exp(jnp.sum(lv[0:1] * lv[1:2], axis=1, keepdims=True))
           - jnp.exp(jnp.sum(lv[2:3] * lv[3:4], axis=1, keepdims=True)) + lambda_init)
    o = acc_ref[0] / l_ref[0] - lam * (acc_ref[1] / l_ref[1])
    y = o * lax.rsqrt(jnp.mean(o * o, axis=1, keepdims=True) + NORM_EPS)
    o_ref[...] = (y * g_ref[...] * (1.0 - lambda_init)).astype(o_ref.dtype)


def _diff_attention(qkv, lam_vecs, subln_g, batch, seq, heads, lambda_init, *, t=512):
    T = qkv.shape[0]
    hw = 2 * DA_HEAD_DIM
    D = heads * hw
    nq = seq // t
    vmem = (4 * _nbytes((seq, hw), bf16) + 4 * _nbytes((t, hw), bf16)
            + 2 * _nbytes((t, hw), f32) + 4 * _nbytes((t, LANES), f32) + 8 * _nbytes((t, t), f32))
    return pl.pallas_call(
        functools.partial(_diff_attn_body, t=t, scale=DA_HEAD_DIM ** -0.5,
                          lambda_init=lambda_init),
        out_shape=jax.ShapeDtypeStruct((T, D), bf16),
        grid=(batch, heads, nq),
        in_specs=[pl.BlockSpec((t, hw), lambda b, h, qi: (b * nq + qi, h)),
                  pl.BlockSpec((seq, hw), lambda b, h, qi: (b, heads + h)),
                  pl.BlockSpec((seq, hw), lambda b, h, qi: (b, 2 * heads + h)),
                  pl.BlockSpec((4, DA_HEAD_DIM), lambda b, h, qi: (0, 0)),
                  pl.BlockSpec((1, hw), lambda b, h, qi: (0, 0))],
        out_specs=pl.BlockSpec((t, hw), lambda b, h, qi: (b * nq + qi, h)),
        scratch_shapes=[pltpu.VMEM((2, t, hw), f32), pltpu.VMEM((2, t, 1), f32),
                        pltpu.VMEM((2, t, 1), f32)],
        compiler_params=_params(3, vmem),
        name="diff_attention",
    )(qkv, qkv, qkv, lam_vecs, subln_g.reshape(1, hw))


HALO = 32


def _dwconv_body(u_ref, halo_ref, w_ref, bdw_ref, g_ref, b_ref, o_ref, win_ref, cv_ref,
                 *, ts, dc, n_dc, rb):
    i, c = pl.program_id(1), pl.program_id(2)
    halo = halo_ref[...]
    win_ref[0:HALO, :] = jnp.where(i > 0, halo, jnp.zeros_like(halo))
    win_ref[HALO:, :] = u_ref[...]
    first = HALO - (CONV_WIDTH - 1)
    for r0 in range(0, ts, rb):
        acc = jnp.zeros((rb, dc), f32) + bdw_ref[...]
        for tap in range(CONV_WIDTH):
            acc = acc + w_ref[tap:tap + 1, :] * win_ref[r0 + first + tap:r0 + first + tap + rb, :]
        cv_ref[c, r0:r0 + rb, :] = acc

    @pl.when(c == n_dc - 1)
    def _():
        d = n_dc * dc
        tot = jnp.zeros((ts, 1), f32)
        for k in range(n_dc):
            tot = tot + jnp.sum(cv_ref[k], axis=1, keepdims=True)
        mu = tot / d
        sq = jnp.zeros((ts, 1), f32)
        for k in range(n_dc):
            zc = cv_ref[k] - mu
            sq = sq + jnp.sum(zc * zc, axis=1, keepdims=True)
        rstd = lax.rsqrt(sq / d + NORM_EPS)
        for k in range(n_dc):
            cols = slice(k * dc, (k + 1) * dc)
            y = (cv_ref[k] - mu) * rstd * g_ref[:, cols] + b_ref[:, cols]
            o_ref[:, cols] = (y * jax.nn.sigmoid(y)).astype(o_ref.dtype)


def _dwconv_ln_swish(u, w_dw, b_dw, ln_g, ln_b, batch, seq, *, ts=256, dc=512, rb=64):
    T, D = u.shape
    n_dc = D // dc
    n_ts = seq // ts
    hb = ts // HALO
    u3 = u.reshape(batch, seq, D)
    vmem = (2 * _nbytes((ts + HALO, dc), f32) * 2 + _nbytes((n_dc, ts, dc), f32)
            + 2 * _nbytes((ts, D), bf16) + 6 * _nbytes((ts, dc), f32))
    out = pl.pallas_call(
        functools.partial(_dwconv_body, ts=ts, dc=dc, n_dc=n_dc, rb=rb),
        out_shape=jax.ShapeDtypeStruct((batch, seq, D), bf16),
        grid=(batch, n_ts, n_dc),
        in_specs=[pl.BlockSpec((None, ts, dc), lambda b, i, c: (b, i, c)),
                  pl.BlockSpec((None, HALO, dc),
                               lambda b, i, c: (b, jnp.maximum(i * hb - 1, 0), c)),
                  pl.BlockSpec((CONV_WIDTH, dc), lambda b, i, c: (0, c)),
                  pl.BlockSpec((1, dc), lambda b, i, c: (0, c)),
                  pl.BlockSpec((1, D), lambda b, i, c: (0, 0)),
                  pl.BlockSpec((1, D), lambda b, i, c: (0, 0))],
        out_specs=pl.BlockSpec((None, ts, D), lambda b, i, c: (b, i, 0)),
        scratch_shapes=[pltpu.VMEM((ts + HALO, dc), f32), pltpu.VMEM((n_dc, ts, dc), f32)],
        compiler_params=_params(3, vmem),
        name="dwconv_ln_swish",
    )(u3, u3, w_dw, b_dw.reshape(1, D), ln_g.reshape(1, D), ln_b.reshape(1, D))
    return out.reshape(T, D)


def _gather_rows_body(idx_ref, src_ref, o_ref, buf_ref, sem, *, tg):
    base = pl.program_id(0) * tg

    def row_copy(r):
        return pltpu.make_async_copy(
            src_ref.at[pl.ds(idx_ref[base + r], 1), :], buf_ref.at[pl.ds(r, 1), :], sem)

    def start(r, carry):
        row_copy(r).start()
        return carry

    def wait(r, carry):
        row_copy(r).wait()
        return carry

    lax.fori_loop(0, tg, start, 0)
    lax.fori_loop(0, tg, wait, 0)
    o_ref[...] = buf_ref[...].astype(o_ref.dtype)


def _gather_rows(src, idx, out_dtype, *, tg=256):
    P = idx.shape[0]
    D = src.shape[1]
    return pl.pallas_call(
        functools.partial(_gather_rows_body, tg=tg),
        out_shape=jax.ShapeDtypeStruct((P, D), out_dtype),
        grid_spec=pltpu.PrefetchScalarGridSpec(
            num_scalar_prefetch=1, grid=(P // tg,),
            in_specs=[pl.BlockSpec(memory_space=pl.ANY)],
            out_specs=pl.BlockSpec((tg, D), lambda i, idx: (i, 0)),
            scratch_shapes=[pltpu.VMEM((tg, D), f32), pltpu.SemaphoreType.DMA(())]),
        compiler_params=_params(1, 4 * _nbytes((tg, D), f32)),
        name="gather_rows",
    )(idx, src)


def _moe_combine_body(pos_ref, x_ref, ye_ref, wts_ref, gate_ref, g_ref, b_ref, xo_ref,
                      buf_ref, sem, *, tm, alpha):
    base = pl.program_id(0) * tm

    def row_copy(r, k):
        return pltpu.make_async_copy(
            ye_ref.at[pl.ds(pos_ref[(base + r) * TOP_K + k], 1), :],
            buf_ref.at[k, pl.ds(r, 1), :], sem.at[k])

    def start(r, carry):
        for k in range(TOP_K):
            row_copy(r, k).start()
        return carry

    def wait(r, carry):
        for k in range(TOP_K):
            row_copy(r, k).wait()
        return carry

    lax.fori_loop(0, tm, start, 0)
    lax.fori_loop(0, tm, wait, 0)
    wts = wts_ref[...]
    y = wts[:, 0:1] * buf_ref[0] + wts[:, 1:2] * buf_ref[1]
    z = alpha * x_ref[...] + (1.0 + gate_ref[...]) * y
    xo_ref[...] = _layer_norm_rows(z, g_ref[...], b_ref[...])


def _moe_combine_deepnorm(x2, ye, pos, wts, gate, g, b, seq, alpha, *, tm=256):
    T, D = x2.shape
    per_b = seq // tm
    return pl.pallas_call(
        functools.partial(_moe_combine_body, tm=tm, alpha=alpha),
        out_shape=jax.ShapeDtypeStruct((T, D), f32),
        grid_spec=pltpu.PrefetchScalarGridSpec(
            num_scalar_prefetch=1, grid=(T // tm,),
            in_specs=[pl.BlockSpec((tm, D), lambda i, p: (i, 0)),
                      pl.BlockSpec(memory_space=pl.ANY),
                      pl.BlockSpec((tm, LANES), lambda i, p: (i, 0)),
                      pl.BlockSpec((None, 1, D), lambda i, p: (i // per_b, 0, 0)),
                      pl.BlockSpec((1, D), lambda i, p: (0, 0)),
                      pl.BlockSpec((1, D), lambda i, p: (0, 0))],
            out_specs=pl.BlockSpec((tm, D), lambda i, p: (i, 0)),
            scratch_shapes=[pltpu.VMEM((TOP_K, tm, D), f32),
                            pltpu.SemaphoreType.DMA((TOP_K,))]),
        compiler_params=_params(1, 9 * _nbytes((tm, D), f32)),
        name="moe_combine_deepnorm",
    )(pos, x2, ye, wts, gate, g.reshape(1, D), b.reshape(1, D))


def _route_tables(top_i, tm):
    n_assign = top_i.size
    n_tok = top_i.shape[0]
    P = n_assign + N_EXPERTS * tm
    n_tiles = P // tm
    flat_e = top_i.reshape(-1)
    onehot = (flat_e[:, None] == jnp.arange(N_EXPERTS, dtype=jnp.int32)[None, :]).astype(jnp.int32)
    csum = jnp.cumsum(onehot, axis=0)
    rank = jnp.sum(onehot * (csum - 1), axis=1)
    counts = csum[-1]
    tiles_e = (counts + tm - 1) // tm
    tile_end = jnp.cumsum(tiles_e)
    starts = (tile_end - tiles_e) * tm
    pos = (starts[flat_e] + rank).astype(jnp.int32)
    token = jnp.arange(n_assign, dtype=jnp.int32) // (n_assign // n_tok)
    row_token = jnp.zeros((P,), jnp.int32).at[pos].set(token)
    n_used = tile_end[-1].astype(jnp.int32)
    tile_ids = jnp.minimum(jnp.arange(n_tiles, dtype=jnp.int32), n_used - 1)
    tile_expert = jnp.minimum(
        jnp.sum((tile_ids[:, None] >= tile_end[None, :]).astype(jnp.int32), axis=1),
        N_EXPERTS - 1).astype(jnp.int32)
    return pos, row_token, tile_expert, n_used.reshape(1)


def kernel(x, c, positions, ada_w, ada_b, ln_g, ln_b, attn_w_qkv, attn_lambda, attn_subln_g, attn_w_o, conv_w_in, conv_b_in, conv_w_dw, conv_b_dw, conv_ln_g, conv_ln_b, conv_w_out, conv_b_out, ffn_w_in, ffn_w_out, moe_w_router, moe_w_in, moe_w_out):
    B, S, D = x.shape
    T = B * S
    depth = ada_w.shape[0]
    alpha = (2 * depth) ** 0.25
    heads = D // (2 * DA_HEAD_DIM)
    rot_dim = DA_HEAD_DIM // 4
    d_ff = ffn_w_out.shape[1]
    d_ffe = moe_w_out.shape[2]

    mods = _adaln(c, ada_w, ada_b)

    def mod(i, s):
        m = mods[2 * i + s]
        return [m[:, k * D:(k + 1) * D].reshape(B, 1, D) for k in range(3)]

    cos_t, sin_up, sin_dn = _rope_tables(positions, rot_dim)
    x2 = x.reshape(T, D)
    shift, scale, gate = mod(0, 0)
    h = _modulate(x2, scale, shift, S)

    for i in range(depth):
        jm = i // N_MIXERS
        if i % N_MIXERS == 0:
            lambda_init = 0.8 - 0.6 * math.exp(-0.3 * i)
            tn = 512
            tab = lambda j, kk, r: (r, 0)
            qkv = _linear(
                h, [(attn_w_qkv[jm], 0)],
                functools.partial(_epi_rope, n_rope_blocks=2 * D // tn, rot_half=rot_dim // 2),
                3 * D, bf16, tm=1024, tn=tn, name="qkv_rope",
                extras=[(t, (1024, LANES), tab) for t in (cos_t, sin_up, sin_dn)])
            o = _diff_attention(qkv, attn_lambda[jm], attn_subln_g[jm], B, S, heads, lambda_init)
            y = _linear(o, [(attn_w_o[jm], 0)], _epi_plain, D, f32, tm=1024, tn=512,
                        name="attn_out")
        else:
            nb = D // 256
            u = _linear(h, [(conv_w_in[jm], 0), (conv_w_in[jm], nb)], _epi_glu, D, f32,
                        tm=1024, tn=256, name="conv_in_glu",
                        biases=[(conv_b_in[jm].reshape(1, 2 * D), 0),
                                (conv_b_in[jm].reshape(1, 2 * D), nb)])
            a = _dwconv_ln_swish(u, conv_w_dw[jm], conv_b_dw[jm], conv_ln_g[jm], conv_ln_b[jm],
                                 B, S)
            y = _linear(a, [(conv_w_out[jm], 0)], _epi_plain, D, f32, tm=1024, tn=512,
                        name="conv_out", biases=[(conv_b_out[jm].reshape(1, D), 0)])

        shift, scale, _ = mod(i, 1)
        jf = i // FFN_PERIOD
        is_moe = i % FFN_PERIOD != 0
        if not is_moe:
            x2, h = _deepnorm(x2, y, gate, ln_g[i, 0], ln_b[i, 0], S, alpha, nxt=(scale, shift))
        else:
            w_router = jnp.zeros((D, LANES), f32).at[:, :N_EXPERTS].set(moe_w_router[jf])
            x2, h32, idx, wts = _deepnorm(x2, y, gate, ln_g[i, 0], ln_b[i, 0], S, alpha,
                                          nxt=(scale, shift), router=w_router)
        _, _, gate = mod(i, 1)

        if not is_moe:
            nb = d_ff // 256
            act = _linear(h, [(ffn_w_in[jf], 0), (ffn_w_in[jf], nb)], _epi_swiglu, d_ff, bf16,
                          tm=1024, tn=256, name="ffn_in_swiglu")
            y = _linear(act, [(ffn_w_out[jf], 0)], _epi_plain, D, f32, tm=512, tn=512,
                        n_k=4, name="ffn_out")
        else:
            tm_e = 512
            pos, row_token, tile_expert, n_used = _route_tables(idx[:, :TOP_K], tm_e)
            hs = _gather_rows(h32, row_token, bf16)
            nb = d_ffe // 256
            act = _linear(hs, [(moe_w_in[jf], 0), (moe_w_in[jf], nb)], _epi_swiglu, d_ffe, bf16,
                          tm=tm_e, tn=256, name="moe_in_swiglu", group=(tile_expert, n_used))
            ye = _linear(act, [(moe_w_out[jf], 0)], _epi_plain, D, f32, tm=tm_e, tn=512,
                         name="moe_out", group=(tile_expert, n_used))

        last = i == depth - 1
        if is_moe:
            x2 = _moe_combine_deepnorm(x2, ye, pos, wts, gate, ln_g[i, 1], ln_b[i, 1], S, alpha)
            if not last:
                shift, scale, gate = mod(i + 1, 0)
                h = _modulate(x2, scale, shift, S)
        elif last:
            (x2,) = _deepnorm(x2, y, gate, ln_g[i, 1], ln_b[i, 1], S, alpha)
        else:
            shift, scale, gate_next = mod(i + 1, 0)
            x2, h = _deepnorm(x2, y, gate, ln_g[i, 1], ln_b[i, 1], S, alpha, nxt=(scale, shift))
            gate = gate_next
    return x2.reshape(B, S, D)
```

```python
import functools
import math

import jax
import jax.numpy as jnp
from jax import lax
from jax.experimental import pallas as pl
from jax.experimental.pallas import tpu as pltpu

DA_HEAD_DIM = 128
ROPE_THETA = 500000.0
CONV_WIDTH = 31
N_EXPERTS = 8
TOP_K = 2
NORM_EPS = 1e-5
N_MIXERS = 2
FFN_PERIOD = 2

LANES = 128
SUBLANES = 8
V7X_VMEM_BYTES = 64 * 1024 * 1024
VMEM_REQUEST_CAP = V7X_VMEM_BYTES - 6 * 1024 * 1024
NEG_BIG = -1e30

f32 = jnp.float32
bf16 = jnp.bfloat16


def _params(n_grid, vmem_bytes):
    limit = min(int(vmem_bytes * 1.2) + (2 << 20), VMEM_REQUEST_CAP)
    return pltpu.CompilerParams(
        dimension_semantics=("arbitrary",) * n_grid, vmem_limit_bytes=limit)


def _nbytes(shape, dtype):
    return math.prod(shape) * jnp.dtype(dtype).itemsize


def _mm_body(*refs, n_pref, n_w, n_b, n_x, n_k, row_split, epilogue):
    pref = refs[:n_pref]
    x_ref = refs[n_pref]
    w_refs = refs[n_pref + 1:n_pref + 1 + n_w]
    b_refs = refs[n_pref + 1 + n_w:n_pref + 1 + n_w + n_b]
    e_refs = refs[n_pref + 1 + n_w + n_b:n_pref + 1 + n_w + n_b + n_x]
    o_ref = refs[n_pref + 1 + n_w + n_b + n_x]
    scratch = refs[n_pref + 2 + n_w + n_b + n_x:]
    wbf_ref = scratch[0]
    acc_ref = scratch[1] if n_k > 1 else None
    j, kk, i = pl.program_id(0), pl.program_id(1), pl.program_id(2)
    tm = x_ref.shape[0]

    if n_pref:
        te_ref, nu_ref = pref
        fresh = (i == 0) | (te_ref[i] != te_ref[jnp.maximum(i - 1, 0)])
        valid = i < nu_ref[0]
    else:
        fresh = i == 0
        valid = None

    @pl.when(fresh)
    def _():
        for n in range(n_w):
            wbf_ref[n] = w_refs[n][...].astype(bf16)

    def compute():
        if n_k == 1:
            sub = tm // row_split
            for s in range(row_split):
                rs = slice(s * sub, (s + 1) * sub)
                x = x_ref[rs, :]
                accs = [jnp.dot(x, wbf_ref[n], preferred_element_type=f32) for n in range(n_w)]
                epilogue(accs, b_refs, e_refs, o_ref, rs)
            return
        x = x_ref[...]
        accs = [jnp.dot(x, wbf_ref[n], preferred_element_type=f32) for n in range(n_w)]
        rows = pl.ds(pl.multiple_of(i * tm, tm), tm)

        @pl.when(kk == 0)
        def _():
            for n in range(n_w):
                acc_ref[n, rows, :] = accs[n]

        @pl.when((kk > 0) & (kk < n_k - 1))
        def _():
            for n in range(n_w):
                acc_ref[n, rows, :] += accs[n]

        @pl.when(kk == n_k - 1)
        def _():
            epilogue([acc_ref[n, rows, :] + accs[n] for n in range(n_w)],
                     b_refs, e_refs, o_ref, slice(0, tm))

    if valid is None:
        compute()
    else:
        pl.when(valid)(compute)

        @pl.when(jnp.logical_not(valid))
        def _():
            o_ref[...] = jnp.zeros(o_ref.shape, o_ref.dtype)


def _linear(x, weights, epilogue, out_cols, out_dtype, *, tm, tn, n_k=1, row_split=1,
            biases=(), extras=(), group=None, name):
    M, K = x.shape
    tk = K // n_k
    assert M % tm == 0 and K % n_k == 0 and out_cols % tn == 0
    n_w, n_b = len(weights), len(biases)
    n_pref = 2 if group is not None else 0
    grid = (out_cols // tn, n_k, M // tm)
    last_k = n_k - 1

    if group is None:
        def row(i, *pref):
            return i
    else:
        def row(i, te, nu):
            return jnp.minimum(i, nu[0] - 1)

    in_specs = [pl.BlockSpec((tm, tk), lambda j, kk, i, *p: (row(i, *p), kk))]
    args = [x]
    for w, off in weights:
        if group is None:
            in_specs.append(pl.BlockSpec(
                (tk, tn), lambda j, kk, i, *p, off=off: (kk, off + j)))
        else:
            in_specs.append(pl.BlockSpec(
                (None, tk, tn), lambda j, kk, i, te, nu, off=off: (te[i], kk, off + j)))
        args.append(w)
    for b, off in biases:
        in_specs.append(pl.BlockSpec((1, tn), lambda j, kk, i, *p, off=off: (0, off + j)))
        args.append(b)
    for arr, blk, imap in extras:
        in_specs.append(pl.BlockSpec(blk, imap))
        args.append(arr)

    if n_k == 1:
        out_map = lambda j, kk, i, *p: (i, j)
    else:
        assert group is None
        out_map = lambda j, kk, i: (jnp.where(kk == last_k, i, 0), j)
    out_spec = pl.BlockSpec((tm, tn), out_map)

    scratch = [pltpu.VMEM((n_w, tk, tn), bf16)]
    vmem = (2 * _nbytes((tm, tk), bf16) + 2 * n_w * _nbytes((tk, tn), f32)
            + n_w * _nbytes((tk, tn), bf16) + 2 * _nbytes((tm, tn), out_dtype)
            + 3 * n_w * _nbytes((tm, tn), f32))
    for arr, blk, _ in extras:
        vmem += 2 * _nbytes([d for d in blk if d is not None], arr.dtype)
    if n_k > 1:
        scratch.append(pltpu.VMEM((n_w, M, tn), f32))
        vmem += n_w * _nbytes((M, tn), f32)

    body = functools.partial(_mm_body, n_pref=n_pref, n_w=n_w, n_b=n_b, n_x=len(extras),
                             n_k=n_k, row_split=row_split, epilogue=epilogue)
    call = pl.pallas_call(
        body,
        out_shape=jax.ShapeDtypeStruct((M, out_cols), out_dtype),
        grid_spec=pltpu.PrefetchScalarGridSpec(
            num_scalar_prefetch=n_pref, grid=grid, in_specs=in_specs, out_specs=out_spec,
            scratch_shapes=scratch),
        compiler_params=_params(3, vmem),
        name=name,
    )
    pref_args = list(group) if group is not None else []
    return call(*pref_args, *args)


def _epi_plain(accs, b_refs, e_refs, o_ref, rs):
    y = accs[0]
    if b_refs:
        y = y + b_refs[0][...]
    o_ref[rs, :] = y.astype(o_ref.dtype)


def _epi_swiglu(accs, b_refs, e_refs, o_ref, rs):
    g, u = accs
    o_ref[rs, :] = (g * jax.nn.sigmoid(g) * u).astype(o_ref.dtype)


def _epi_glu(accs, b_refs, e_refs, o_ref, rs):
    a = accs[0] + b_refs[0][...]
    g = accs[1] + b_refs[1][...]
    o_ref[rs, :] = (a * jax.nn.sigmoid(g)).astype(o_ref.dtype)


def _epi_rope(accs, b_refs, e_refs, o_ref, rs, *, rot_half):
    y = accs[0]
    c, su, sd = (e[rs, :] for e in e_refs)
    for g in range(y.shape[1] // LANES):
        yg = y[:, g * LANES:(g + 1) * LANES]
        up = pltpu.roll(yg, rot_half, axis=1)
        dn = pltpu.roll(yg, LANES - rot_half, axis=1)
        o_ref[rs, g * LANES:(g + 1) * LANES] = (yg * c + up * su + dn * sd).astype(o_ref.dtype)


def _adaln_body(c_ref, w_ref, b_ref, o_ref):
    c = c_ref[...]
    sc = (c * jax.nn.sigmoid(c)).astype(bf16)
    y = jnp.dot(sc, w_ref[...].astype(bf16), preferred_element_type=f32)
    o_ref[...] = y + b_ref[...]


def _adaln(c, ada_w, ada_b, *, tn=512):
    B, D = c.shape
    L = ada_w.shape[0] * ada_w.shape[1]
    w = ada_w.reshape(L, D, 3 * D)
    b = ada_b.reshape(L, 1, 3 * D)
    rows = 8
    c_pad = jnp.zeros((rows, D), f32).at[:B].set(c)
    vmem = 2 * _nbytes((D, tn), f32) + _nbytes((D, tn), bf16) + 4 * _nbytes((rows, D), f32)
    out = pl.pallas_call(
        _adaln_body,
        out_shape=jax.ShapeDtypeStruct((L, rows, 3 * D), f32),
        grid=(L, 3 * D // tn),
        in_specs=[pl.BlockSpec((rows, D), lambda m, j: (0, 0)),
                  pl.BlockSpec((None, D, tn), lambda m, j: (m, 0, j)),
                  pl.BlockSpec((None, 1, tn), lambda m, j: (m, 0, j))],
        out_specs=pl.BlockSpec((None, rows, tn), lambda m, j: (m, 0, j)),
        compiler_params=_params(2, vmem),
        name="adaln",
    )(c_pad, w, b)
    return out[:, :B]


def _modulate_body(x_ref, scale_ref, shift_ref, h_ref):
    h_ref[...] = (x_ref[...] * (1.0 + scale_ref[...]) + shift_ref[...]).astype(h_ref.dtype)


def _modulate(x2, scale, shift, seq, *, tm=512):
    T, D = x2.shape
    per_b = seq // tm
    vec = pl.BlockSpec((None, 1, D), lambda i: (i // per_b, 0, 0))
    return pl.pallas_call(
        _modulate_body,
        out_shape=jax.ShapeDtypeStruct((T, D), bf16),
        grid=(T // tm,),
        in_specs=[pl.BlockSpec((tm, D), lambda i: (i, 0)), vec, vec],
        out_specs=pl.BlockSpec((tm, D), lambda i: (i, 0)),
        compiler_params=_params(1, 2 * _nbytes((tm, D), f32) * 2),
        name="modulate",
    )(x2, scale, shift)


def _layer_norm_rows(z, g, b):
    mu = jnp.mean(z, axis=-1, keepdims=True)
    zc = z - mu
    var = jnp.mean(zc * zc, axis=-1, keepdims=True)
    return zc * lax.rsqrt(var + NORM_EPS) * g + b


def _top2_route(logits, n_experts):
    lane_i = lax.broadcasted_iota(jnp.int32, logits.shape, 1)
    lane = lane_i.astype(f32)
    lg = jnp.where(lane_i < n_experts, logits, -jnp.inf)
    m1 = jnp.max(lg, axis=1, keepdims=True)
    i1 = jnp.min(jnp.where(lg == m1, lane, float(LANES)), axis=1, keepdims=True)
    lg2 = jnp.where(lane == i1, -jnp.inf, lg)
    m2 = jnp.max(lg2, axis=1, keepdims=True)
    i2 = jnp.min(jnp.where(lg2 == m2, lane, float(LANES)), axis=1, keepdims=True)
    e = jnp.exp(m2 - m1)
    w1 = 1.0 / (1.0 + e)
    w2 = e / (1.0 + e)
    idx = jnp.where(lane_i == 0, i1, jnp.where(lane_i == 1, i2, 0.0)).astype(jnp.int32)
    wts = jnp.where(lane_i == 0, w1, jnp.where(lane_i == 1, w2, 0.0))
    return idx, wts


def _deepnorm_body(*refs, alpha, mode):
    x_ref, y_ref, gate_ref, g_ref, b_ref = refs[:5]
    z = alpha * x_ref[...] + (1.0 + gate_ref[...]) * y_ref[...]
    xo = _layer_norm_rows(z, g_ref[...], b_ref[...])
    if mode == "last":
        refs[5][...] = xo
        return
    scale_ref, shift_ref = refs[5:7]
    h = xo * (1.0 + scale_ref[...]) + shift_ref[...]
    if mode == "next":
        xo_ref, h_ref = refs[7:9]
        xo_ref[...] = xo
        h_ref[...] = h.astype(h_ref.dtype)
        return
    wr_ref, xo_ref, h_ref, idx_ref, wts_ref = refs[7:12]
    xo_ref[...] = xo
    h_ref[...] = h
    logits = jnp.dot(h, wr_ref[...], preferred_element_type=f32,
                     precision=lax.Precision.HIGHEST)
    idx, wts = _top2_route(logits, N_EXPERTS)
    idx_ref[...] = idx
    wts_ref[...] = wts


def _deepnorm(x2, y2, gate, g, b, seq, alpha, *, nxt=None, router=None, tm=256):
    T, D = x2.shape
    per_b = seq // tm
    rowblk = pl.BlockSpec((tm, D), lambda i: (i, 0))
    vec = pl.BlockSpec((None, 1, D), lambda i: (i // per_b, 0, 0))
    par = pl.BlockSpec((1, D), lambda i: (0, 0))
    lane_blk = pl.BlockSpec((tm, LANES), lambda i: (i, 0))
    in_specs = [rowblk, rowblk, vec, par, par]
    args = [x2, y2, gate, g.reshape(1, D), b.reshape(1, D)]
    out_shape = [jax.ShapeDtypeStruct((T, D), f32)]
    out_specs = [rowblk]
    n_blk = 3
    mode = "last"
    if nxt is not None:
        mode = "next"
        in_specs += [vec, vec]
        args += list(nxt)
        out_shape.append(jax.ShapeDtypeStruct((T, D), bf16 if router is None else f32))
        out_specs.append(rowblk)
        n_blk = 4
    if router is not None:
        mode = "router"
        in_specs.append(pl.BlockSpec((D, LANES), lambda i: (0, 0)))
        args.append(router)
        out_shape += [jax.ShapeDtypeStruct((T, LANES), jnp.int32),
                      jax.ShapeDtypeStruct((T, LANES), f32)]
        out_specs += [lane_blk, lane_blk]
    vmem = (2 * n_blk + 3) * _nbytes((tm, D), f32) + 2 * _nbytes((D, LANES), f32)
    return pl.pallas_call(
        functools.partial(_deepnorm_body, alpha=alpha, mode=mode),
        out_shape=out_shape,
        grid=(T // tm,),
        in_specs=in_specs,
        out_specs=out_specs,
        compiler_params=_params(1, vmem),
        name="deepnorm_" + mode,
    )(*args)


def _rope_table_body(pos_ref, invf_ref, cos_ref, sin_up_ref, sin_dn_ref, *, rot_half):
    ang = pos_ref[...] * invf_ref[...]
    lane = lax.broadcasted_iota(jnp.int32, ang.shape, 1)
    c, s = jnp.cos(ang), jnp.sin(ang)
    cos_ref[0] = jnp.where(lane < 2 * rot_half, c, 1.0)
    sin_up_ref[0] = jnp.where((lane >= rot_half) & (lane < 2 * rot_half), s, 0.0)
    sin_dn_ref[0] = jnp.where(lane < rot_half, -s, 0.0)
    cos_ref[1] = jnp.ones(ang.shape, f32)
    sin_up_ref[1] = jnp.zeros(ang.shape, f32)
    sin_dn_ref[1] = jnp.zeros(ang.shape, f32)


def _rope_tables(positions, rot_dim, *, tm=1024):
    T = positions.size
    half = rot_dim // 2
    inv_freq = 1.0 / (ROPE_THETA ** (jnp.arange(0, rot_dim, 2, dtype=f32) / rot_dim))
    invf = jnp.zeros((1, LANES), f32).at[0, :rot_dim].set(jnp.tile(inv_freq, 2))
    pos = positions.astype(f32).reshape(T, 1)
    tab = pl.BlockSpec((2, tm, LANES), lambda i: (0, i, 0))
    return pl.pallas_call(
        functools.partial(_rope_table_body, rot_half=half),
        out_shape=[jax.ShapeDtypeStruct((2, T, LANES), f32)] * 3,
        grid=(T // tm,),
        in_specs=[pl.BlockSpec((tm, 1), lambda i: (i, 0)),
                  pl.BlockSpec((1, LANES), lambda i: (0, 0))],
        out_specs=[tab, tab, tab],
        compiler_params=_params(1, 12 * _nbytes((tm, LANES), f32)),
        name="rope_tables",
    )(pos, invf)


def _diff_attn_body(q_ref, k_ref, v_ref, lam_ref, g_ref, o_ref, acc_ref, m_ref, l_ref,
                    *, tq, tk, scale, lambda_init):
    qi = pl.program_id(2)
    dh = DA_HEAD_DIM
    c2 = scale * math.log2(math.e)
    k_reps = tk // LANES
    v_reps = 2 * dh // LANES
    n_diag = tq // tk
    m_ref[...] = jnp.full(m_ref.shape, NEG_BIG, f32)
    l_ref[...] = jnp.zeros(l_ref.shape, f32)
    acc_ref[...] = jnp.zeros(acc_ref.shape, f32)

    def step(ki, diag):
        rows = pl.ds(pl.multiple_of(ki * tk, tk), tk)
        v = v_ref[rows, :]
        for c in range(2):
            s = lax.dot_general(q_ref[:, c * dh:(c + 1) * dh], k_ref[rows, c * dh:(c + 1) * dh],
                                (((1,), (1,)), ((), ())), preferred_element_type=f32)
            if diag is not None:
                r = lax.broadcasted_iota(jnp.int32, s.shape, 0)
                col = lax.broadcasted_iota(jnp.int32, s.shape, 1)
                s = jnp.where(col + diag * tk <= r, s, NEG_BIG)
            m_old = m_ref[c]
            m_new = jnp.maximum(m_old, jnp.max(s, axis=1, keepdims=True))
            a = jnp.exp2((m_old - m_new) * c2)
            p = jnp.exp2((s - jnp.tile(m_new, (1, k_reps))) * c2)
            l_ref[c] = a * l_ref[c] + jnp.sum(p, axis=1, keepdims=True)
            acc_ref[c] = (jnp.tile(a, (1, v_reps)) * acc_ref[c]
                          + jnp.dot(p.astype(bf16), v, preferred_element_type=f32))
            m_ref[c] = m_new

    def off_diag(ki, carry):
        step(ki, None)
        return carry

    lax.fori_loop(0, qi * n_diag, off_diag, 0)
    for d in range(n_diag):
        step(qi * n_diag + d, d)

    lv = lam_ref[...]
    lam = (jnp.exp(jnp.sum(lv[0:1] * lv[1:2], axis=1, keepdims=True))
           - jnp.exp(jnp.sum(lv[2:3] * lv[3:4], axis=1, keepdims=True)) + lambda_init)
    o = (acc_ref[0] / jnp.tile(l_ref[0], (1, v_reps))
         - lam * (acc_ref[1] / jnp.tile(l_ref[1], (1, v_reps))))
    y = o * lax.rsqrt(jnp.mean(o * o, axis=1, keepdims=True) + NORM_EPS)
    o_ref[...] = (y * g_ref[...] * (1.0 - lambda_init)).astype(o_ref.dtype)


def _diff_attention(qkv, lam_vecs, subln_g, batch, seq, heads, lambda_init, *, tq=512, tk=512):
    T = qkv.shape[0]
    hw = 2 * DA_HEAD_DIM
    D = heads * hw
    nq = seq // tq
    vmem = (4 * _nbytes((seq, hw), bf16) + 4 * _nbytes((tq, hw), bf16)
            + 2 * _nbytes((tq, hw), f32) + 4 * _nbytes((tq, LANES), f32)
            + 8 * _nbytes((tq, tk), f32))
    return pl.pallas_call(
        functools.partial(_diff_attn_body, tq=tq, tk=tk, scale=DA_HEAD_DIM ** -0.5,
                          lambda_init=lambda_init),
        out_shape=jax.ShapeDtypeStruct((T, D), bf16),
        grid=(batch, heads, nq),
        in_specs=[pl.BlockSpec((tq, hw), lambda b, h, qi: (b * nq + qi, h)),
                  pl.BlockSpec((seq, hw), lambda b, h, qi: (b, heads + h)),
                  pl.BlockSpec((seq, hw), lambda b, h, qi: (b, 2 * heads + h)),
                  pl.BlockSpec((4, DA_HEAD_DIM), lambda b, h, qi: (0, 0)),
                  pl.BlockSpec((1, hw), lambda b, h, qi: (0, 0))],
        out_specs=pl.BlockSpec((tq, hw), lambda b, h, qi: (b * nq + qi, h)),
        scratch_shapes=[pltpu.VMEM((2, tq, hw), f32), pltpu.VMEM((2, tq, LANES), f32),
                        pltpu.VMEM((2, tq, LANES), f32)],
        compiler_params=_params(3, vmem),
        name="diff_attention",
    )(qkv, qkv, qkv, lam_vecs, subln_g.reshape(1, hw))


HALO = 32


def _dwconv_body(u_ref, halo_ref, w_ref, bdw_ref, g_ref, b_ref, o_ref, win_ref, cv_ref,
                 *, ts, dc, n_dc, rb):
    i, c = pl.program_id(1), pl.program_id(2)
    halo = halo_ref[...]
    win_ref[0, 0:HALO, :] = jnp.where(i > 0, halo, jnp.zeros_like(halo))
    win_ref[0, HALO:, :] = u_ref[...]
    first = HALO - (CONV_WIDTH - 1)
    span = ts + HALO - SUBLANES
    for r in range(1, SUBLANES):
        win_ref[r, 0:span, :] = win_ref[0, r:r + span, :]
    for r0 in range(0, ts, rb):
        acc = jnp.zeros((rb, dc), f32) + bdw_ref[...]
        for tap in range(CONV_WIDTH):
            r = (first + tap) % SUBLANES
            base = r0 + first + tap - r
            acc = acc + w_ref[tap:tap + 1, :] * win_ref[r, base:base + rb, :]
        cv_ref[c, r0:r0 + rb, :] = acc

    @pl.when(c == n_dc - 1)
    def _():
        d = n_dc * dc
        tot = jnp.zeros((ts, 1), f32)
        for k in range(n_dc):
            tot = tot + jnp.sum(cv_ref[k], axis=1, keepdims=True)
        mu = tot / d
        sq = jnp.zeros((ts, 1), f32)
        for k in range(n_dc):
            zc = cv_ref[k] - mu
            sq = sq + jnp.sum(zc * zc, axis=1, keepdims=True)
        rstd = lax.rsqrt(sq / d + NORM_EPS)
        for k in range(n_dc):
            cols = slice(k * dc, (k + 1) * dc)
            y = (cv_ref[k] - mu) * rstd * g_ref[:, cols] + b_ref[:, cols]
            o_ref[:, cols] = (y * jax.nn.sigmoid(y)).astype(o_ref.dtype)


def _dwconv_ln_swish(u, w_dw, b_dw, ln_g, ln_b, batch, seq, *, ts=256, dc=512, rb=64):
    T, D = u.shape
    n_dc = D // dc
    n_ts = seq // ts
    hb = ts // HALO
    u3 = u.reshape(batch, seq, D)
    vmem = ((4 + SUBLANES) * _nbytes((ts + HALO, dc), f32) + _nbytes((n_dc, ts, dc), f32)
            + 2 * _nbytes((ts, D), bf16) + 6 * _nbytes((ts, dc), f32))
    out = pl.pallas_call(
        functools.partial(_dwconv_body, ts=ts, dc=dc, n_dc=n_dc, rb=rb),
        out_shape=jax.ShapeDtypeStruct((batch, seq, D), bf16),
        grid=(batch, n_ts, n_dc),
        in_specs=[pl.BlockSpec((None, ts, dc), lambda b, i, c: (b, i, c)),
                  pl.BlockSpec((None, HALO, dc),
                               lambda b, i, c: (b, jnp.maximum(i * hb - 1, 0), c)),
                  pl.BlockSpec((CONV_WIDTH, dc), lambda b, i, c: (0, c)),
                  pl.BlockSpec((1, dc), lambda b, i, c: (0, c)),
                  pl.BlockSpec((1, D), lambda b, i, c: (0, 0)),
                  pl.BlockSpec((1, D), lambda b, i, c: (0, 0))],
        out_specs=pl.BlockSpec((None, ts, D), lambda b, i, c: (b, i, 0)),
        scratch_shapes=[pltpu.VMEM((SUBLANES, ts + HALO, dc), f32),
                        pltpu.VMEM((n_dc, ts, dc), f32)],
        compiler_params=_params(3, vmem),
        name="dwconv_ln_swish",
    )(u3, u3, w_dw, b_dw.reshape(1, D), ln_g.reshape(1, D), ln_b.reshape(1, D))
    return out.reshape(T, D)


DMA_ISSUE_UNROLL = 8


def _gather_rows_body(idx_ref, src_ref, o_ref, buf_ref, sem, *, tg, n_steps):
    i = pl.program_id(0)

    def row_copy(step, r, slot):
        return pltpu.make_async_copy(
            src_ref.at[pl.ds(idx_ref[step * tg + r], 1), :],
            buf_ref.at[slot, pl.ds(r, 1), :], sem.at[slot])

    def start_tile(step):
        def start(r, carry):
            row_copy(step, r, step % 2).start()
            return carry
        lax.fori_loop(0, tg, start, 0, unroll=DMA_ISSUE_UNROLL)

    @pl.when(i == 0)
    def _():
        start_tile(i)

    @pl.when(i + 1 < n_steps)
    def _():
        start_tile(i + 1)

    def wait(r, carry):
        row_copy(i, r, i % 2).wait()
        return carry

    lax.fori_loop(0, tg, wait, 0, unroll=DMA_ISSUE_UNROLL)
    o_ref[...] = buf_ref[i % 2].astype(o_ref.dtype)


def _gather_rows(src, idx, out_dtype, *, tg=256):
    P = idx.shape[0]
    D = src.shape[1]
    n_steps = P // tg
    return pl.pallas_call(
        functools.partial(_gather_rows_body, tg=tg, n_steps=n_steps),
        out_shape=jax.ShapeDtypeStruct((P, D), out_dtype),
        grid_spec=pltpu.PrefetchScalarGridSpec(
            num_scalar_prefetch=1, grid=(n_steps,),
            in_specs=[pl.BlockSpec(memory_space=pl.ANY)],
            out_specs=pl.BlockSpec((tg, D), lambda i, idx: (i, 0)),
            scratch_shapes=[pltpu.VMEM((2, tg, D), f32), pltpu.SemaphoreType.DMA((2,))]),
        compiler_params=_params(1, 6 * _nbytes((tg, D), f32)),
        name="gather_rows",
    )(idx, src)


def _moe_combine_body(pos_ref, x_ref, ye_ref, wts_ref, gate_ref, g_ref, b_ref, xo_ref,
                      buf_ref, sem, *, tm, n_steps, alpha):
    i = pl.program_id(0)

    def row_copy(step, r, k, slot):
        return pltpu.make_async_copy(
            ye_ref.at[pl.ds(pos_ref[(step * tm + r) * TOP_K + k], 1), :],
            buf_ref.at[slot, k, pl.ds(r, 1), :], sem.at[slot])

    def start_tile(step):
        def start(r, carry):
            for k in range(TOP_K):
                row_copy(step, r, k, step % 2).start()
            return carry
        lax.fori_loop(0, tm, start, 0, unroll=DMA_ISSUE_UNROLL)

    @pl.when(i == 0)
    def _():
        start_tile(i)

    @pl.when(i + 1 < n_steps)
    def _():
        start_tile(i + 1)

    def wait(r, carry):
        for k in range(TOP_K):
            row_copy(i, r, k, i % 2).wait()
        return carry

    lax.fori_loop(0, tm, wait, 0, unroll=DMA_ISSUE_UNROLL)
    wts = wts_ref[...]
    y = wts[:, 0:1] * buf_ref[i % 2, 0] + wts[:, 1:2] * buf_ref[i % 2, 1]
    z = alpha * x_ref[...] + (1.0 + gate_ref[...]) * y
    xo_ref[...] = _layer_norm_rows(z, g_ref[...], b_ref[...])


def _moe_combine_deepnorm(x2, ye, pos, wts, gate, g, b, seq, alpha, *, tm=256):
    T, D = x2.shape
    per_b = seq // tm
    return pl.pallas_call(
        functools.partial(_moe_combine_body, tm=tm, n_steps=T // tm, alpha=alpha),
        out_shape=jax.ShapeDtypeStruct((T, D), f32),
        grid_spec=pltpu.PrefetchScalarGridSpec(
            num_scalar_prefetch=1, grid=(T // tm,),
            in_specs=[pl.BlockSpec((tm, D), lambda i, p: (i, 0)),
                      pl.BlockSpec(memory_space=pl.ANY),
                      pl.BlockSpec((tm, LANES), lambda i, p: (i, 0)),
                      pl.BlockSpec((None, 1, D), lambda i, p: (i // per_b, 0, 0)),
                      pl.BlockSpec((1, D), lambda i, p: (0, 0)),
                      pl.BlockSpec((1, D), lambda i, p: (0, 0))],
            out_specs=pl.BlockSpec((tm, D), lambda i, p: (i, 0)),
            scratch_shapes=[pltpu.VMEM((2, TOP_K, tm, D), f32),
                            pltpu.SemaphoreType.DMA((2,))]),
        compiler_params=_params(1, 11 * _nbytes((tm, D), f32)),
        name="moe_combine_deepnorm",
    )(pos, x2, ye, wts, gate, g.reshape(1, D), b.reshape(1, D))


def _route_tables(top_i, tm):
    n_assign = top_i.size
    n_tok = top_i.shape[0]
    P = n_assign + N_EXPERTS * tm
    n_tiles = P // tm
    flat_e = top_i.reshape(-1)
    onehot = (flat_e[:, None] == jnp.arange(N_EXPERTS, dtype=jnp.int32)[None, :]).astype(jnp.int32)
    csum = jnp.cumsum(onehot, axis=0)
    rank = jnp.sum(onehot * (csum - 1), axis=1)
    counts = csum[-1]
    tiles_e = (counts + tm - 1) // tm
    tile_end = jnp.cumsum(tiles_e)
    starts = (tile_end - tiles_e) * tm
    pos = (starts[flat_e] + rank).astype(jnp.int32)
    token = jnp.arange(n_assign, dtype=jnp.int32) // (n_assign // n_tok)
    row_token = jnp.zeros((P,), jnp.int32).at[pos].set(token)
    n_used = tile_end[-1].astype(jnp.int32)
    tile_ids = jnp.minimum(jnp.arange(n_tiles, dtype=jnp.int32), n_used - 1)
    tile_expert = jnp.minimum(
        jnp.sum((tile_ids[:, None] >= tile_end[None, :]).astype(jnp.int32), axis=1),
        N_EXPERTS - 1).astype(jnp.int32)
    return pos, row_token, tile_expert, n_used.reshape(1)


def kernel(x, c, positions, ada_w, ada_b, ln_g, ln_b, attn_w_qkv, attn_lambda, attn_subln_g, attn_w_o, conv_w_in, conv_b_in, conv_w_dw, conv_b_dw, conv_ln_g, conv_ln_b, conv_w_out, conv_b_out, ffn_w_in, ffn_w_out, moe_w_router, moe_w_in, moe_w_out):
    B, S, D = x.shape
    T = B * S
    depth = ada_w.shape[0]
    alpha = (2 * depth) ** 0.25
    heads = D // (2 * DA_HEAD_DIM)
    rot_dim = DA_HEAD_DIM // 4
    d_ff = ffn_w_out.shape[1]
    d_ffe = moe_w_out.shape[2]

    mods = _adaln(c, ada_w, ada_b)

    def mod(i, s):
        m = mods[2 * i + s]
        return [m[:, k * D:(k + 1) * D].reshape(B, 1, D) for k in range(3)]

    cos_t, sin_up, sin_dn = _rope_tables(positions, rot_dim)
    x2 = x.reshape(T, D)
    shift, scale, gate = mod(0, 0)
    h = _modulate(x2, scale, shift, S)

    for i in range(depth):
        jm = i // N_MIXERS
        if i % N_MIXERS == 0:
            lambda_init = 0.8 - 0.6 * math.exp(-0.3 * i)
            tn = 512
            n_rope_blocks = 2 * D // tn
            tab = lambda j, kk, r: (jnp.where(j < n_rope_blocks, 0, 1), r, 0)
            qkv = _linear(
                h, [(attn_w_qkv[jm], 0)], functools.partial(_epi_rope, rot_half=rot_dim // 2),
                3 * D, bf16, tm=1024, tn=tn, row_split=4, name="qkv_rope",
                extras=[(t, (None, 1024, LANES), tab) for t in (cos_t, sin_up, sin_dn)])
            o = _diff_attention(qkv, attn_lambda[jm], attn_subln_g[jm], B, S, heads, lambda_init)
            y = _linear(o, [(attn_w_o[jm], 0)], _epi_plain, D, f32, tm=1024, tn=512,
                        name="attn_out")
        else:
            nb = D // 256
            u = _linear(h, [(conv_w_in[jm], 0), (conv_w_in[jm], nb)], _epi_glu, D, f32,
                        tm=1024, tn=256, name="conv_in_glu",
                        biases=[(conv_b_in[jm].reshape(1, 2 * D), 0),
                                (conv_b_in[jm].reshape(1, 2 * D), nb)])
            a = _dwconv_ln_swish(u, conv_w_dw[jm], conv_b_dw[jm], conv_ln_g[jm], conv_ln_b[jm],
                                 B, S)
            y = _linear(a, [(conv_w_out[jm], 0)], _epi_plain, D, f32, tm=1024, tn=512,
                        name="conv_out", biases=[(conv_b_out[jm].reshape(1, D), 0)])

        shift, scale, _ = mod(i, 1)
        jf = i // FFN_PERIOD
        is_moe = i % FFN_PERIOD != 0
        if not is_moe:
            x2, h = _deepnorm(x2, y, gate, ln_g[i, 0], ln_b[i, 0], S, alpha, nxt=(scale, shift))
        else:
            w_router = jnp.zeros((D, LANES), f32).at[:, :N_EXPERTS].set(moe_w_router[jf])
            x2, h32, idx, wts = _deepnorm(x2, y, gate, ln_g[i, 0], ln_b[i, 0], S, alpha,
                                          nxt=(scale, shift), router=w_router)
        _, _, gate = mod(i, 1)

        if not is_moe:
            nb = d_ff // 256
            act = _linear(h, [(ffn_w_in[jf], 0), (ffn_w_in[jf], nb)], _epi_swiglu, d_ff, bf16,
                          tm=1024, tn=256, name="ffn_in_swiglu")
            y = _linear(act, [(ffn_w_out[jf], 0)], _epi_plain, D, f32, tm=512, tn=512,
                        n_k=4, name="ffn_out")
        else:
            tm_e = 256
            pos, row_token, tile_expert, n_used = _route_tables(idx[:, :TOP_K], tm_e)
            hs = _gather_rows(h32, row_token, bf16)
            nb = d_ffe // 512
            act = _linear(hs, [(moe_w_in[jf], 0), (moe_w_in[jf], nb)], _epi_swiglu, d_ffe, bf16,
                          tm=tm_e, tn=512, name="moe_in_swiglu", group=(tile_expert, n_used))
            ye = _linear(act, [(moe_w_out[jf], 0)], _epi_plain, D, f32, tm=tm_e, tn=1024,
                         name="moe_out", group=(tile_expert, n_used))

        last = i == depth - 1
        if is_moe:
            x2 = _moe_combine_deepnorm(x2, ye, pos, wts, gate, ln_g[i, 1], ln_b[i, 1], S, alpha)
            if not last:
                shift, scale, gate = mod(i + 1, 0)
                h = _modulate(x2, scale, shift, S)
        elif last:
            (x2,) = _deepnorm(x2, y, gate, ln_g[i, 1], ln_b[i, 1], S, alpha)
        else:
            shift, scale, gate_next = mod(i + 1, 0)
            x2, h = _deepnorm(x2, y, gate, ln_g[i, 1], ln_b[i, 1], S, alpha, nxt=(scale, shift))
            gate = gate_next
    return x2.reshape(B, S, D)
```

```python
import functools
import math

import jax
import jax.numpy as jnp
from jax import lax
from jax.experimental import pallas as pl
from jax.experimental.pallas import tpu as pltpu

DA_HEAD_DIM = 128
ROPE_THETA = 500000.0
CONV_WIDTH = 31
N_EXPERTS = 8
TOP_K = 2
NORM_EPS = 1e-5
N_MIXERS = 2
FFN_PERIOD = 2

LANES = 128
SUBLANES = 8
V7X_VMEM_BYTES = 64 * 1024 * 1024
VMEM_REQUEST_CAP = V7X_VMEM_BYTES - 6 * 1024 * 1024
NEG_BIG = -1e30

f32 = jnp.float32
bf16 = jnp.bfloat16


def _params(n_grid, vmem_bytes):
    limit = min(int(vmem_bytes * 1.2) + (2 << 20), VMEM_REQUEST_CAP)
    return pltpu.CompilerParams(
        dimension_semantics=("arbitrary",) * n_grid, vmem_limit_bytes=limit)


def _nbytes(shape, dtype):
    return math.prod(shape) * jnp.dtype(dtype).itemsize


def _mm_body(*refs, n_pref, n_w, n_b, n_x, n_k, n_j, tk, tn, w_offs, row_split, epilogue):
    pref = refs[:n_pref]
    x_ref = refs[n_pref]
    w_refs = refs[n_pref + 1:n_pref + 1 + n_w]
    b_refs = refs[n_pref + 1 + n_w:n_pref + 1 + n_w + n_b]
    e_refs = refs[n_pref + 1 + n_w + n_b:n_pref + 1 + n_w + n_b + n_x]
    o_ref = refs[n_pref + 1 + n_w + n_b + n_x]
    scratch = refs[n_pref + 2 + n_w + n_b + n_x:]
    stage_ref, wbf_ref, sem = scratch[:3]
    acc_ref = scratch[3] if n_k > 1 else None
    j, kk, i = pl.program_id(0), pl.program_id(1), pl.program_id(2)
    tm = x_ref.shape[0]

    if n_pref:
        te_ref, nu_ref, nx_ref = pref
        e_cur = te_ref[i]
        fresh = (i == 0) | (e_cur != te_ref[jnp.maximum(i - 1, 0)])
        valid = i < nu_ref[0]
        within = nx_ref[i] >= 0
        e_next = jnp.where(within, nx_ref[i], te_ref[0])
    else:
        e_cur = e_next = None
        fresh = i == 0
        valid = None
        within = jnp.bool_(False)
    more_k = kk + 1 < n_k
    stay = jnp.logical_or(within, more_k)
    kk_next = jnp.where(within, kk, jnp.where(more_k, kk + 1, 0))
    j_next = jnp.where(stay, j, j + 1)
    has_next = jnp.logical_or(stay, j + 1 < n_j)

    def w_copy(n, e, kk_, j_):
        src = w_refs[n] if e is None else w_refs[n].at[e]
        src = src.at[pl.ds(pl.multiple_of(kk_ * tk, tk), tk),
                     pl.ds(pl.multiple_of((w_offs[n] + j_) * tn, tn), tn)]
        return pltpu.make_async_copy(src, stage_ref.at[n], sem.at[n])

    @pl.when(fresh)
    def _():
        @pl.when((j == 0) & (kk == 0) & (i == 0))
        def _():
            for n in range(n_w):
                w_copy(n, e_cur, kk, j).start()

        for n in range(n_w):
            w_copy(n, e_cur, kk, j).wait()
            wbf_ref[n] = stage_ref[n].astype(bf16)

        @pl.when(has_next)
        def _():
            for n in range(n_w):
                w_copy(n, e_next, kk_next, j_next).start()

    def compute():
        if n_k == 1:
            sub = tm // row_split
            for s in range(row_split):
                rs = slice(s * sub, (s + 1) * sub)
                x = x_ref[rs, :]
                accs = [jnp.dot(x, wbf_ref[n], preferred_element_type=f32) for n in range(n_w)]
                epilogue(accs, b_refs, e_refs, o_ref, rs)
            return
        rows = pl.ds(pl.multiple_of(i * tm, tm), tm)

        def products():
            x = x_ref[...]
            return [jnp.dot(x, wbf_ref[n], preferred_element_type=f32) for n in range(n_w)]

        @pl.when(kk == 0)
        def _():
            for n, a in enumerate(products()):
                acc_ref[n, rows, :] = a

        @pl.when((kk > 0) & (kk < n_k - 1))
        def _():
            for n, a in enumerate(products()):
                acc_ref[n, rows, :] += a

        @pl.when(kk == n_k - 1)
        def _():
            epilogue([acc_ref[n, rows, :] + a for n, a in enumerate(products())],
                     b_refs, e_refs, o_ref, slice(0, tm))

    if valid is None:
        compute()
    else:
        pl.when(valid)(compute)

        @pl.when(jnp.logical_not(valid))
        def _():
            o_ref[...] = jnp.zeros(o_ref.shape, o_ref.dtype)


def _linear(x, weights, epilogue, out_cols, out_dtype, *, tm, tn, n_k=1, row_split=1,
            biases=(), extras=(), group=None, name):
    M, K = x.shape
    tk = K // n_k
    assert M % tm == 0 and K % n_k == 0 and out_cols % tn == 0 and tm % row_split == 0
    n_w, n_b = len(weights), len(biases)
    n_pref = len(group) if group is not None else 0
    n_j = out_cols // tn
    grid = (n_j, n_k, M // tm)
    last_k = n_k - 1

    if group is None:
        def row(i, *pref):
            return i
    else:
        def row(i, te, nu, nx):
            return jnp.minimum(i, nu[0] - 1)

    in_specs = [pl.BlockSpec((tm, tk), lambda j, kk, i, *p: (row(i, *p), kk))]
    args = [x]
    for w, _ in weights:
        in_specs.append(pl.BlockSpec(memory_space=pl.ANY))
        args.append(w)
    for b, off in biases:
        in_specs.append(pl.BlockSpec((1, tn), lambda j, kk, i, *p, off=off: (0, off + j)))
        args.append(b)
    for arr, blk, imap in extras:
        in_specs.append(pl.BlockSpec(blk, imap))
        args.append(arr)

    if n_k == 1:
        out_map = lambda j, kk, i, *p: (i, j)
    else:
        assert group is None
        out_map = lambda j, kk, i: (jnp.where(kk == last_k, i, 0), j)
    out_spec = pl.BlockSpec((tm, tn), out_map)

    scratch = [pltpu.VMEM((n_w, tk, tn), f32), pltpu.VMEM((n_w, tk, tn), bf16),
               pltpu.SemaphoreType.DMA((n_w,))]
    vmem = (2 * _nbytes((tm, tk), bf16) + n_w * _nbytes((tk, tn), f32)
            + n_w * _nbytes((tk, tn), bf16) + 2 * _nbytes((tm, tn), out_dtype)
            + 3 * n_w * _nbytes((tm // row_split, tn), f32))
    for arr, blk, _ in extras:
        vmem += 2 * _nbytes([d for d in blk if d is not None], arr.dtype)
    if n_k > 1:
        scratch.append(pltpu.VMEM((n_w, M, tn), f32))
        vmem += n_w * _nbytes((M, tn), f32)

    body = functools.partial(_mm_body, n_pref=n_pref, n_w=n_w, n_b=n_b, n_x=len(extras),
                             n_k=n_k, n_j=n_j, tk=tk, tn=tn,
                             w_offs=[off for _, off in weights], row_split=row_split,
                             epilogue=epilogue)
    call = pl.pallas_call(
        body,
        out_shape=jax.ShapeDtypeStruct((M, out_cols), out_dtype),
        grid_spec=pltpu.PrefetchScalarGridSpec(
            num_scalar_prefetch=n_pref, grid=grid, in_specs=in_specs, out_specs=out_spec,
            scratch_shapes=scratch),
        compiler_params=_params(3, vmem),
        name=name,
    )
    pref_args = list(group) if group is not None else []
    return call(*pref_args, *args)


def _epi_plain(accs, b_refs, e_refs, o_ref, rs):
    y = accs[0]
    if b_refs:
        y = y + b_refs[0][...]
    o_ref[rs, :] = y.astype(o_ref.dtype)


def _epi_swiglu(accs, b_refs, e_refs, o_ref, rs):
    g, u = accs
    o_ref[rs, :] = (g * jax.nn.sigmoid(g) * u).astype(o_ref.dtype)


def _epi_glu(accs, b_refs, e_refs, o_ref, rs):
    a = accs[0] + b_refs[0][...]
    g = accs[1] + b_refs[1][...]
    o_ref[rs, :] = (a * jax.nn.sigmoid(g)).astype(o_ref.dtype)


def _epi_rope(accs, b_refs, e_refs, o_ref, rs, *, rot_half):
    y = accs[0]
    c, su, sd = (e[rs, :] for e in e_refs)
    for g in range(y.shape[1] // LANES):
        yg = y[:, g * LANES:(g + 1) * LANES]
        up = pltpu.roll(yg, rot_half, axis=1)
        dn = pltpu.roll(yg, LANES - rot_half, axis=1)
        o_ref[rs, g * LANES:(g + 1) * LANES] = (yg * c + up * su + dn * sd).astype(o_ref.dtype)


def _adaln_body(c_ref, w_ref, b_ref, o_ref):
    c = c_ref[...]
    sc = (c * jax.nn.sigmoid(c)).astype(bf16)
    y = jnp.dot(sc, w_ref[...].astype(bf16), preferred_element_type=f32)
    o_ref[...] = y + b_ref[...]


def _adaln(c, ada_w, ada_b, *, tn=512):
    B, D = c.shape
    L = ada_w.shape[0] * ada_w.shape[1]
    w = ada_w.reshape(L, D, 3 * D)
    b = ada_b.reshape(L, 1, 3 * D)
    rows = 8
    c_pad = jnp.zeros((rows, D), f32).at[:B].set(c)
    vmem = 2 * _nbytes((D, tn), f32) + _nbytes((D, tn), bf16) + 4 * _nbytes((rows, D), f32)
    out = pl.pallas_call(
        _adaln_body,
        out_shape=jax.ShapeDtypeStruct((L, rows, 3 * D), f32),
        grid=(L, 3 * D // tn),
        in_specs=[pl.BlockSpec((rows, D), lambda m, j: (0, 0)),
                  pl.BlockSpec((None, D, tn), lambda m, j: (m, 0, j)),
                  pl.BlockSpec((None, 1, tn), lambda m, j: (m, 0, j))],
        out_specs=pl.BlockSpec((None, rows, tn), lambda m, j: (m, 0, j)),
        compiler_params=_params(2, vmem),
        name="adaln",
    )(c_pad, w, b)
    return out[:, :B]


def _modulate_body(x_ref, scale_ref, shift_ref, h_ref):
    h_ref[...] = (x_ref[...] * (1.0 + scale_ref[...]) + shift_ref[...]).astype(h_ref.dtype)


def _modulate(x2, scale, shift, seq, *, tm=512):
    T, D = x2.shape
    per_b = seq // tm
    vec = pl.BlockSpec((None, 1, D), lambda i: (i // per_b, 0, 0))
    return pl.pallas_call(
        _modulate_body,
        out_shape=jax.ShapeDtypeStruct((T, D), bf16),
        grid=(T // tm,),
        in_specs=[pl.BlockSpec((tm, D), lambda i: (i, 0)), vec, vec],
        out_specs=pl.BlockSpec((tm, D), lambda i: (i, 0)),
        compiler_params=_params(1, 2 * _nbytes((tm, D), f32) * 2),
        name="modulate",
    )(x2, scale, shift)


def _layer_norm_rows(z, g, b):
    mu = jnp.mean(z, axis=-1, keepdims=True)
    zc = z - mu
    var = jnp.mean(zc * zc, axis=-1, keepdims=True)
    return zc * lax.rsqrt(var + NORM_EPS) * g + b


def _top2_route(logits, n_experts):
    lane_i = lax.broadcasted_iota(jnp.int32, logits.shape, 1)
    lane = lane_i.astype(f32)
    lg = jnp.where(lane_i < n_experts, logits, -jnp.inf)
    m1 = jnp.max(lg, axis=1, keepdims=True)
    i1 = jnp.min(jnp.where(lg == m1, lane, float(LANES)), axis=1, keepdims=True)
    lg2 = jnp.where(lane == i1, -jnp.inf, lg)
    m2 = jnp.max(lg2, axis=1, keepdims=True)
    i2 = jnp.min(jnp.where(lg2 == m2, lane, float(LANES)), axis=1, keepdims=True)
    e = jnp.exp(m2 - m1)
    w1 = 1.0 / (1.0 + e)
    w2 = e / (1.0 + e)
    idx = jnp.where(lane_i == 0, i1, jnp.where(lane_i == 1, i2, 0.0)).astype(jnp.int32)
    wts = jnp.where(lane_i == 0, w1, jnp.where(lane_i == 1, w2, 0.0))
    return idx, wts


def _deepnorm_body(*refs, alpha, mode):
    x_ref, y_ref, gate_ref, g_ref, b_ref = refs[:5]
    z = alpha * x_ref[...] + (1.0 + gate_ref[...]) * y_ref[...]
    xo = _layer_norm_rows(z, g_ref[...], b_ref[...])
    if mode == "last":
        refs[5][...] = xo
        return
    scale_ref, shift_ref = refs[5:7]
    h = xo * (1.0 + scale_ref[...]) + shift_ref[...]
    if mode == "next":
        xo_ref, h_ref = refs[7:9]
        xo_ref[...] = xo
        h_ref[...] = h.astype(h_ref.dtype)
        return
    wr_ref, xo_ref, h_ref, idx_ref, wts_ref = refs[7:12]
    xo_ref[...] = xo
    h_ref[...] = h
    logits = jnp.dot(h, wr_ref[...], preferred_element_type=f32,
                     precision=lax.Precision.HIGHEST)
    idx, wts = _top2_route(logits, N_EXPERTS)
    idx_ref[...] = idx
    wts_ref[...] = wts


def _deepnorm(x2, y2, gate, g, b, seq, alpha, *, nxt=None, router=None, tm=256):
    T, D = x2.shape
    per_b = seq // tm
    rowblk = pl.BlockSpec((tm, D), lambda i: (i, 0))
    vec = pl.BlockSpec((None, 1, D), lambda i: (i // per_b, 0, 0))
    par = pl.BlockSpec((1, D), lambda i: (0, 0))
    lane_blk = pl.BlockSpec((tm, LANES), lambda i: (i, 0))
    in_specs = [rowblk, rowblk, vec, par, par]
    args = [x2, y2, gate, g.reshape(1, D), b.reshape(1, D)]
    out_shape = [jax.ShapeDtypeStruct((T, D), f32)]
    out_specs = [rowblk]
    n_blk = 3
    mode = "last"
    if nxt is not None:
        mode = "next"
        in_specs += [vec, vec]
        args += list(nxt)
        out_shape.append(jax.ShapeDtypeStruct((T, D), bf16 if router is None else f32))
        out_specs.append(rowblk)
        n_blk = 4
    if router is not None:
        mode = "router"
        in_specs.append(pl.BlockSpec((D, LANES), lambda i: (0, 0)))
        args.append(router)
        out_shape += [jax.ShapeDtypeStruct((T, LANES), jnp.int32),
                      jax.ShapeDtypeStruct((T, LANES), f32)]
        out_specs += [lane_blk, lane_blk]
    vmem = (2 * n_blk + 3) * _nbytes((tm, D), f32) + 2 * _nbytes((D, LANES), f32)
    return pl.pallas_call(
        functools.partial(_deepnorm_body, alpha=alpha, mode=mode),
        out_shape=out_shape,
        grid=(T // tm,),
        in_specs=in_specs,
        out_specs=out_specs,
        compiler_params=_params(1, vmem),
        name="deepnorm_" + mode,
    )(*args)


def _rope_table_body(pos_ref, invf_ref, cos_ref, sin_up_ref, sin_dn_ref, *, rot_half):
    ang = pos_ref[...] * invf_ref[...]
    lane = lax.broadcasted_iota(jnp.int32, ang.shape, 1)
    c, s = jnp.cos(ang), jnp.sin(ang)
    cos_ref[0] = jnp.where(lane < 2 * rot_half, c, 1.0)
    sin_up_ref[0] = jnp.where((lane >= rot_half) & (lane < 2 * rot_half), s, 0.0)
    sin_dn_ref[0] = jnp.where(lane < rot_half, -s, 0.0)
    cos_ref[1] = jnp.ones(ang.shape, f32)
    sin_up_ref[1] = jnp.zeros(ang.shape, f32)
    sin_dn_ref[1] = jnp.zeros(ang.shape, f32)


def _rope_tables(positions, rot_dim, *, tm=1024):
    T = positions.size
    half = rot_dim // 2
    inv_freq = 1.0 / (ROPE_THETA ** (jnp.arange(0, rot_dim, 2, dtype=f32) / rot_dim))
    invf = jnp.zeros((1, LANES), f32).at[0, :rot_dim].set(jnp.tile(inv_freq, 2))
    pos = positions.astype(f32).reshape(T, 1)
    tab = pl.BlockSpec((2, tm, LANES), lambda i: (0, i, 0))
    return pl.pallas_call(
        functools.partial(_rope_table_body, rot_half=half),
        out_shape=[jax.ShapeDtypeStruct((2, T, LANES), f32)] * 3,
        grid=(T // tm,),
        in_specs=[pl.BlockSpec((tm, 1), lambda i: (i, 0)),
                  pl.BlockSpec((1, LANES), lambda i: (0, 0))],
        out_specs=[tab, tab, tab],
        compiler_params=_params(1, 12 * _nbytes((tm, LANES), f32)),
        name="rope_tables",
    )(pos, invf)


def _diff_attn_body(q_ref, k_ref, v_ref, lam_ref, g_ref, o_ref, acc_ref, m_ref, l_ref,
                    *, tq, tk, scale, lambda_init):
    qi = pl.program_id(2)
    dh = DA_HEAD_DIM
    c2 = scale * math.log2(math.e)
    k_reps = tk // LANES
    v_reps = 2 * dh // LANES
    n_diag = tq // tk
    m_ref[...] = jnp.full(m_ref.shape, NEG_BIG, f32)
    l_ref[...] = jnp.zeros(l_ref.shape, f32)
    acc_ref[...] = jnp.zeros(acc_ref.shape, f32)

    def step(ki, diag):
        rows = pl.ds(pl.multiple_of(ki * tk, tk), tk)
        v = v_ref[rows, :]
        for c in range(2):
            s = lax.dot_general(q_ref[:, c * dh:(c + 1) * dh], k_ref[rows, c * dh:(c + 1) * dh],
                                (((1,), (1,)), ((), ())), preferred_element_type=f32)
            if diag is not None:
                r = lax.broadcasted_iota(jnp.int32, s.shape, 0)
                col = lax.broadcasted_iota(jnp.int32, s.shape, 1)
                s = jnp.where(col + diag * tk <= r, s, NEG_BIG)
            m_old = m_ref[c]
            m_new = jnp.maximum(m_old, jnp.max(s, axis=1, keepdims=True))
            a = jnp.exp2((m_old - m_new) * c2)
            p = jnp.exp2((s - jnp.tile(m_new, (1, k_reps))) * c2)
            l_ref[c] = a * l_ref[c] + jnp.sum(p, axis=1, keepdims=True)
            acc_ref[c] = (jnp.tile(a, (1, v_reps)) * acc_ref[c]
                          + jnp.dot(p.astype(bf16), v, preferred_element_type=f32))
            m_ref[c] = m_new

    n_below = qi * n_diag

    def below_pair(pi, carry):
        step(2 * pi, None)
        step(2 * pi + 1, None)
        return carry

    lax.fori_loop(0, n_below // 2, below_pair, 0)

    @pl.when(n_below % 2 == 1)
    def _():
        step(n_below - 1, None)

    for d in range(n_diag):
        step(n_below + d, d)

    lv = lam_ref[...]
    lam = (jnp.exp(jnp.sum(lv[0:1] * lv[1:2], axis=1, keepdims=True))
           - jnp.exp(jnp.sum(lv[2:3] * lv[3:4], axis=1, keepdims=True)) + lambda_init)
    o = (acc_ref[0] / jnp.tile(l_ref[0], (1, v_reps))
         - lam * (acc_ref[1] / jnp.tile(l_ref[1], (1, v_reps))))
    y = o * lax.rsqrt(jnp.mean(o * o, axis=1, keepdims=True) + NORM_EPS)
    o_ref[...] = (y * g_ref[...] * (1.0 - lambda_init)).astype(o_ref.dtype)


def _diff_attention(qkv, lam_vecs, subln_g, batch, seq, heads, lambda_init, *, tq=512, tk=512):
    T = qkv.shape[0]
    hw = 2 * DA_HEAD_DIM
    D = heads * hw
    nq = seq // tq
    vmem = (4 * _nbytes((seq, hw), bf16) + 4 * _nbytes((tq, hw), bf16)
            + 2 * _nbytes((tq, hw), f32) + 4 * _nbytes((tq, LANES), f32)
            + 8 * _nbytes((tq, tk), f32))
    return pl.pallas_call(
        functools.partial(_diff_attn_body, tq=tq, tk=tk, scale=DA_HEAD_DIM ** -0.5,
                          lambda_init=lambda_init),
        out_shape=jax.ShapeDtypeStruct((T, D), bf16),
        grid=(batch, heads, nq),
        in_specs=[pl.BlockSpec((tq, hw), lambda b, h, qi: (b * nq + qi, h)),
                  pl.BlockSpec((seq, hw), lambda b, h, qi: (b, heads + h)),
                  pl.BlockSpec((seq, hw), lambda b, h, qi: (b, 2 * heads + h)),
                  pl.BlockSpec((4, DA_HEAD_DIM), lambda b, h, qi: (0, 0)),
                  pl.BlockSpec((1, hw), lambda b, h, qi: (0, 0))],
        out_specs=pl.BlockSpec((tq, hw), lambda b, h, qi: (b * nq + qi, h)),
        scratch_shapes=[pltpu.VMEM((2, tq, hw), f32), pltpu.VMEM((2, tq, LANES), f32),
                        pltpu.VMEM((2, tq, LANES), f32)],
        compiler_params=_params(3, vmem),
        name="diff_attention",
    )(qkv, qkv, qkv, lam_vecs, subln_g.reshape(1, hw))


HALO = 32


def _dwconv_body(u_ref, halo_ref, w_ref, bdw_ref, g_ref, b_ref, o_ref, win_ref, cv_ref,
                 *, ts, dc, n_dc, rb):
    i, c = pl.program_id(1), pl.program_id(2)
    halo = halo_ref[...]
    win_ref[0, 0:HALO, :] = jnp.where(i > 0, halo, jnp.zeros_like(halo))
    win_ref[0, HALO:, :] = u_ref[...]
    first = HALO - (CONV_WIDTH - 1)
    span = ts + HALO - SUBLANES
    for r in range(1, SUBLANES):
        win_ref[r, 0:span, :] = win_ref[0, r:r + span, :]
    for r0 in range(0, ts, rb):
        acc = jnp.zeros((rb, dc), f32) + bdw_ref[...]
        for tap in range(CONV_WIDTH):
            r = (first + tap) % SUBLANES
            base = r0 + first + tap - r
            acc = acc + w_ref[tap:tap + 1, :] * win_ref[r, base:base + rb, :]
        cv_ref[c, r0:r0 + rb, :] = acc

    @pl.when(c == n_dc - 1)
    def _():
        d = n_dc * dc
        tot = jnp.zeros((ts, 1), f32)
        for k in range(n_dc):
            tot = tot + jnp.sum(cv_ref[k], axis=1, keepdims=True)
        mu = tot / d
        sq = jnp.zeros((ts, 1), f32)
        for k in range(n_dc):
            zc = cv_ref[k] - mu
            sq = sq + jnp.sum(zc * zc, axis=1, keepdims=True)
        rstd = lax.rsqrt(sq / d + NORM_EPS)
        for k in range(n_dc):
            cols = slice(k * dc, (k + 1) * dc)
            y = (cv_ref[k] - mu) * rstd * g_ref[:, cols] + b_ref[:, cols]
            o_ref[:, cols] = (y * jax.nn.sigmoid(y)).astype(o_ref.dtype)


def _dwconv_ln_swish(u, w_dw, b_dw, ln_g, ln_b, batch, seq, *, ts=256, dc=512, rb=64):
    T, D = u.shape
    n_dc = D // dc
    n_ts = seq // ts
    hb = ts // HALO
    u3 = u.reshape(batch, seq, D)
    vmem = ((4 + SUBLANES) * _nbytes((ts + HALO, dc), f32) + _nbytes((n_dc, ts, dc), f32)
            + 2 * _nbytes((ts, D), bf16) + 6 * _nbytes((ts, dc), f32))
    out = pl.pallas_call(
        functools.partial(_dwconv_body, ts=ts, dc=dc, n_dc=n_dc, rb=rb),
        out_shape=jax.ShapeDtypeStruct((batch, seq, D), bf16),
        grid=(batch, n_ts, n_dc),
        in_specs=[pl.BlockSpec((None, ts, dc), lambda b, i, c: (b, i, c)),
                  pl.BlockSpec((None, HALO, dc),
                               lambda b, i, c: (b, jnp.maximum(i * hb - 1, 0), c)),
                  pl.BlockSpec((CONV_WIDTH, dc), lambda b, i, c: (0, c)),
                  pl.BlockSpec((1, dc), lambda b, i, c: (0, c)),
                  pl.BlockSpec((1, D), lambda b, i, c: (0, 0)),
                  pl.BlockSpec((1, D), lambda b, i, c: (0, 0))],
        out_specs=pl.BlockSpec((None, ts, D), lambda b, i, c: (b, i, 0)),
        scratch_shapes=[pltpu.VMEM((SUBLANES, ts + HALO, dc), f32),
                        pltpu.VMEM((n_dc, ts, dc), f32)],
        compiler_params=_params(3, vmem),
        name="dwconv_ln_swish",
    )(u3, u3, w_dw, b_dw.reshape(1, D), ln_g.reshape(1, D), ln_b.reshape(1, D))
    return out.reshape(T, D)


DMA_ISSUE_UNROLL = 8


def _gather_rows_body(idx_ref, src_ref, o_ref, buf_ref, sem, *, tg, n_steps):
    i = pl.program_id(0)

    def row_copy(step, r, slot):
        return pltpu.make_async_copy(
            src_ref.at[pl.ds(idx_ref[step * tg + r], 1), :],
            buf_ref.at[slot, pl.ds(r, 1), :], sem.at[slot])

    def start_tile(step):
        def start(r, carry):
            row_copy(step, r, step % 2).start()
            return carry
        lax.fori_loop(0, tg, start, 0, unroll=DMA_ISSUE_UNROLL)

    @pl.when(i == 0)
    def _():
        start_tile(i)

    @pl.when(i + 1 < n_steps)
    def _():
        start_tile(i + 1)

    def wait(r, carry):
        row_copy(i, r, i % 2).wait()
        return carry

    lax.fori_loop(0, tg, wait, 0, unroll=DMA_ISSUE_UNROLL)
    o_ref[...] = buf_ref[i % 2].astype(o_ref.dtype)


def _gather_rows(src, idx, out_dtype, *, tg=256):
    P = idx.shape[0]
    D = src.shape[1]
    n_steps = P // tg
    return pl.pallas_call(
        functools.partial(_gather_rows_body, tg=tg, n_steps=n_steps),
        out_shape=jax.ShapeDtypeStruct((P, D), out_dtype),
        grid_spec=pltpu.PrefetchScalarGridSpec(
            num_scalar_prefetch=1, grid=(n_steps,),
            in_specs=[pl.BlockSpec(memory_space=pl.ANY)],
            out_specs=pl.BlockSpec((tg, D), lambda i, idx: (i, 0)),
            scratch_shapes=[pltpu.VMEM((2, tg, D), f32), pltpu.SemaphoreType.DMA((2,))]),
        compiler_params=_params(1, 6 * _nbytes((tg, D), f32)),
        name="gather_rows",
    )(idx, src)


def _moe_combine_body(pos_ref, x_ref, ye_ref, wts_ref, gate_ref, g_ref, b_ref, xo_ref,
                      buf_ref, sem, *, tm, n_steps, alpha):
    i = pl.program_id(0)

    def row_copy(step, r, k, slot):
        return pltpu.make_async_copy(
            ye_ref.at[pl.ds(pos_ref[(step * tm + r) * TOP_K + k], 1), :],
            buf_ref.at[slot, k, pl.ds(r, 1), :], sem.at[slot])

    def start_tile(step):
        def start(r, carry):
            for k in range(TOP_K):
                row_copy(step, r, k, step % 2).start()
            return carry
        lax.fori_loop(0, tm, start, 0, unroll=DMA_ISSUE_UNROLL)

    @pl.when(i == 0)
    def _():
        start_tile(i)

    @pl.when(i + 1 < n_steps)
    def _():
        start_tile(i + 1)

    def wait(r, carry):
        for k in range(TOP_K):
            row_copy(i, r, k, i % 2).wait()
        return carry

    lax.fori_loop(0, tm, wait, 0, unroll=DMA_ISSUE_UNROLL)
    wts = wts_ref[...]
    y = wts[:, 0:1] * buf_ref[i % 2, 0] + wts[:, 1:2] * buf_ref[i % 2, 1]
    z = alpha * x_ref[...] + (1.0 + gate_ref[...]) * y
    xo_ref[...] = _layer_norm_rows(z, g_ref[...], b_ref[...])


def _moe_combine_deepnorm(x2, ye, pos, wts, gate, g, b, seq, alpha, *, tm=256):
    T, D = x2.shape
    per_b = seq // tm
    return pl.pallas_call(
        functools.partial(_moe_combine_body, tm=tm, n_steps=T // tm, alpha=alpha),
        out_shape=jax.ShapeDtypeStruct((T, D), f32),
        grid_spec=pltpu.PrefetchScalarGridSpec(
            num_scalar_prefetch=1, grid=(T // tm,),
            in_specs=[pl.BlockSpec((tm, D), lambda i, p: (i, 0)),
                      pl.BlockSpec(memory_space=pl.ANY),
                      pl.BlockSpec((tm, LANES), lambda i, p: (i, 0)),
                      pl.BlockSpec((None, 1, D), lambda i, p: (i // per_b, 0, 0)),
                      pl.BlockSpec((1, D), lambda i, p: (0, 0)),
                      pl.BlockSpec((1, D), lambda i, p: (0, 0))],
            out_specs=pl.BlockSpec((tm, D), lambda i, p: (i, 0)),
            scratch_shapes=[pltpu.VMEM((2, TOP_K, tm, D), f32),
                            pltpu.SemaphoreType.DMA((2,))]),
        compiler_params=_params(1, 11 * _nbytes((tm, D), f32)),
        name="moe_combine_deepnorm",
    )(pos, x2, ye, wts, gate, g.reshape(1, D), b.reshape(1, D))


def _route_tables(top_i, tm):
    n_assign = top_i.size
    n_tok = top_i.shape[0]
    P = n_assign + N_EXPERTS * tm
    n_tiles = P // tm
    flat_e = top_i.reshape(-1)
    onehot = (flat_e[:, None] == jnp.arange(N_EXPERTS, dtype=jnp.int32)[None, :]).astype(jnp.int32)
    csum = jnp.cumsum(onehot, axis=0)
    rank = jnp.sum(onehot * (csum - 1), axis=1)
    counts = csum[-1]
    tiles_e = (counts + tm - 1) // tm
    tile_end = jnp.cumsum(tiles_e)
    starts = (tile_end - tiles_e) * tm
    pos = (starts[flat_e] + rank).astype(jnp.int32)
    token = jnp.arange(n_assign, dtype=jnp.int32) // (n_assign // n_tok)
    row_token = jnp.zeros((P,), jnp.int32).at[pos].set(token)
    n_used = tile_end[-1].astype(jnp.int32)
    tile_ids = jnp.minimum(jnp.arange(n_tiles, dtype=jnp.int32), n_used - 1)
    tile_expert = jnp.minimum(
        jnp.sum((tile_ids[:, None] >= tile_end[None, :]).astype(jnp.int32), axis=1),
        N_EXPERTS - 1).astype(jnp.int32)
    experts = jnp.arange(N_EXPERTS, dtype=jnp.int32)
    later_nonempty = (experts[None, :] > experts[:, None]) & (tiles_e[None, :] > 0)
    next_e = jnp.min(jnp.where(later_nonempty, experts[None, :], N_EXPERTS), axis=1)
    next_e = jnp.where(next_e < N_EXPERTS, next_e, -1).astype(jnp.int32)
    return pos, row_token, (tile_expert, n_used.reshape(1), next_e[tile_expert])


def kernel(x, c, positions, ada_w, ada_b, ln_g, ln_b, attn_w_qkv, attn_lambda, attn_subln_g, attn_w_o, conv_w_in, conv_b_in, conv_w_dw, conv_b_dw, conv_ln_g, conv_ln_b, conv_w_out, conv_b_out, ffn_w_in, ffn_w_out, moe_w_router, moe_w_in, moe_w_out):
    B, S, D = x.shape
    T = B * S
    depth = ada_w.shape[0]
    alpha = (2 * depth) ** 0.25
    heads = D // (2 * DA_HEAD_DIM)
    rot_dim = DA_HEAD_DIM // 4
    d_ff = ffn_w_out.shape[1]
    d_ffe = moe_w_out.shape[2]

    mods = _adaln(c, ada_w, ada_b)

    def mod(i, s):
        m = mods[2 * i + s]
        return [m[:, k * D:(k + 1) * D].reshape(B, 1, D) for k in range(3)]

    cos_t, sin_up, sin_dn = _rope_tables(positions, rot_dim)
    x2 = x.reshape(T, D)
    shift, scale, gate = mod(0, 0)
    h = _modulate(x2, scale, shift, S)

    for i in range(depth):
        jm = i // N_MIXERS
        if i % N_MIXERS == 0:
            lambda_init = 0.8 - 0.6 * math.exp(-0.3 * i)
            tn = 1024
            n_rope_blocks = 2 * D // tn
            tab = lambda j, kk, r: (jnp.where(j < n_rope_blocks, 0, 1), r, 0)
            qkv = _linear(
                h, [(attn_w_qkv[jm], 0)], functools.partial(_epi_rope, rot_half=rot_dim // 2),
                3 * D, bf16, tm=1024, tn=tn, row_split=4, name="qkv_rope",
                extras=[(t, (None, 1024, LANES), tab) for t in (cos_t, sin_up, sin_dn)])
            o = _diff_attention(qkv, attn_lambda[jm], attn_subln_g[jm], B, S, heads, lambda_init)
            y = _linear(o, [(attn_w_o[jm], 0)], _epi_plain, D, f32, tm=1024, tn=512,
                        name="attn_out")
        else:
            nb = D // 512
            u = _linear(h, [(conv_w_in[jm], 0), (conv_w_in[jm], nb)], _epi_glu, D, f32,
                        tm=1024, tn=512, row_split=2, name="conv_in_glu",
                        biases=[(conv_b_in[jm].reshape(1, 2 * D), 0),
                                (conv_b_in[jm].reshape(1, 2 * D), nb)])
            a = _dwconv_ln_swish(u, conv_w_dw[jm], conv_b_dw[jm], conv_ln_g[jm], conv_ln_b[jm],
                                 B, S)
            y = _linear(a, [(conv_w_out[jm], 0)], _epi_plain, D, f32, tm=1024, tn=512,
                        name="conv_out", biases=[(conv_b_out[jm].reshape(1, D), 0)])

        shift, scale, _ = mod(i, 1)
        jf = i // FFN_PERIOD
        is_moe = i % FFN_PERIOD != 0
        if not is_moe:
            x2, h = _deepnorm(x2, y, gate, ln_g[i, 0], ln_b[i, 0], S, alpha, nxt=(scale, shift))
        else:
            w_router = jnp.zeros((D, LANES), f32).at[:, :N_EXPERTS].set(moe_w_router[jf])
            x2, h32, idx, wts = _deepnorm(x2, y, gate, ln_g[i, 0], ln_b[i, 0], S, alpha,
                                          nxt=(scale, shift), router=w_router)
        _, _, gate = mod(i, 1)

        if not is_moe:
            nb = d_ff // 512
            act = _linear(h, [(ffn_w_in[jf], 0), (ffn_w_in[jf], nb)], _epi_swiglu, d_ff, bf16,
                          tm=1024, tn=512, row_split=2, name="ffn_in_swiglu")
            y = _linear(act, [(ffn_w_out[jf], 0)], _epi_plain, D, f32, tm=1024, tn=512,
                        n_k=4, name="ffn_out")
        else:
            tm_e = 256
            pos, row_token, groups = _route_tables(idx[:, :TOP_K], tm_e)
            hs = _gather_rows(h32, row_token, bf16)
            nb = d_ffe // 512
            act = _linear(hs, [(moe_w_in[jf], 0), (moe_w_in[jf], nb)], _epi_swiglu, d_ffe, bf16,
                          tm=tm_e, tn=512, name="moe_in_swiglu", group=groups)
            ye = _linear(act, [(moe_w_out[jf], 0)], _epi_plain, D, f32, tm=tm_e, tn=1024,
                         name="moe_out", group=groups)

        last = i == depth - 1
        if is_moe:
            x2 = _moe_combine_deepnorm(x2, ye, pos, wts, gate, ln_g[i, 1], ln_b[i, 1], S, alpha)
            if not last:
                shift, scale, gate = mod(i + 1, 0)
                h = _modulate(x2, scale, shift, S)
        elif last:
            (x2,) = _deepnorm(x2, y, gate, ln_g[i, 1], ln_b[i, 1], S, alpha)
        else:
            shift, scale, gate_next = mod(i + 1, 0)
            x2, h = _deepnorm(x2, y, gate, ln_g[i, 1], ln_b[i, 1], S, alpha, nxt=(scale, shift))
            gate = gate_next
    return x2.reshape(B, S, D)
```

```python
import functools
import math

import jax
import jax.numpy as jnp
from jax import lax
from jax.experimental import pallas as pl
from jax.experimental.pallas import tpu as pltpu

DA_HEAD_DIM = 128
ROPE_THETA = 500000.0
CONV_WIDTH = 31
N_EXPERTS = 8
TOP_K = 2
NORM_EPS = 1e-5
N_MIXERS = 2
FFN_PERIOD = 2

LANES = 128
SUBLANES = 8
V7X_VMEM_BYTES = 64 * 1024 * 1024
VMEM_REQUEST_CAP = V7X_VMEM_BYTES - 3 * 1024 * 1024
NEG_BIG = -1e30

f32 = jnp.float32
bf16 = jnp.bfloat16


def _params(n_grid, vmem_bytes):
    limit = min(int(vmem_bytes * 1.2) + (2 << 20), VMEM_REQUEST_CAP)
    return pltpu.CompilerParams(
        dimension_semantics=("arbitrary",) * n_grid, vmem_limit_bytes=limit)


def _nbytes(shape, dtype):
    return math.prod(shape) * jnp.dtype(dtype).itemsize


def _mm_body(*refs, n_pref, n_w, n_b, n_x, n_k, n_j, tk, tn, w_offs, row_split, epilogue):
    pref = refs[:n_pref]
    x_ref = refs[n_pref]
    w_refs = refs[n_pref + 1:n_pref + 1 + n_w]
    b_refs = refs[n_pref + 1 + n_w:n_pref + 1 + n_w + n_b]
    e_refs = refs[n_pref + 1 + n_w + n_b:n_pref + 1 + n_w + n_b + n_x]
    o_ref = refs[n_pref + 1 + n_w + n_b + n_x]
    scratch = refs[n_pref + 2 + n_w + n_b + n_x:]
    stage_ref, wbf_ref, sem = scratch[:3]
    acc_ref = scratch[3] if n_k > 1 else None
    e_refs = tuple(e_refs) + tuple(scratch[4 if n_k > 1 else 3:])
    j, kk, i = pl.program_id(0), pl.program_id(1), pl.program_id(2)
    tm = x_ref.shape[0]

    if n_pref:
        te_ref, nu_ref, nx_ref = pref
        e_cur = te_ref[i]
        fresh = (i == 0) | (e_cur != te_ref[jnp.maximum(i - 1, 0)])
        valid = i < nu_ref[0]
        within = nx_ref[i] >= 0
        e_next = jnp.where(within, nx_ref[i], te_ref[0])
    else:
        e_cur = e_next = None
        fresh = i == 0
        valid = None
        within = jnp.bool_(False)
    more_k = kk + 1 < n_k
    stay = jnp.logical_or(within, more_k)
    kk_next = jnp.where(within, kk, jnp.where(more_k, kk + 1, 0))
    j_next = jnp.where(stay, j, j + 1)
    has_next = jnp.logical_or(stay, j + 1 < n_j)

    def w_copy(n, e, kk_, j_):
        src = w_refs[n] if e is None else w_refs[n].at[e]
        src = src.at[pl.ds(pl.multiple_of(kk_ * tk, tk), tk),
                     pl.ds(pl.multiple_of((w_offs[n] + j_) * tn, tn), tn)]
        return pltpu.make_async_copy(src, stage_ref.at[n], sem.at[n])

    @pl.when(fresh)
    def _():
        @pl.when((j == 0) & (kk == 0) & (i == 0))
        def _():
            for n in range(n_w):
                w_copy(n, e_cur, kk, j).start()

        for n in range(n_w):
            w_copy(n, e_cur, kk, j).wait()
            wbf_ref[n] = stage_ref[n].astype(bf16)

        @pl.when(has_next)
        def _():
            for n in range(n_w):
                w_copy(n, e_next, kk_next, j_next).start()

    def compute():
        if n_k == 1:
            sub = tm // row_split
            for s in range(row_split):
                rs = slice(s * sub, (s + 1) * sub)
                x = x_ref[rs, :]
                accs = [jnp.dot(x, wbf_ref[n], preferred_element_type=f32) for n in range(n_w)]
                epilogue(accs, b_refs, e_refs, o_ref, rs)
            return
        rows = pl.ds(pl.multiple_of(i * tm, tm), tm)

        def products():
            x = x_ref[...]
            return [jnp.dot(x, wbf_ref[n], preferred_element_type=f32) for n in range(n_w)]

        @pl.when(kk == 0)
        def _():
            for n, a in enumerate(products()):
                acc_ref[n, rows, :] = a

        @pl.when((kk > 0) & (kk < n_k - 1))
        def _():
            for n, a in enumerate(products()):
                acc_ref[n, rows, :] += a

        @pl.when(kk == n_k - 1)
        def _():
            epilogue([acc_ref[n, rows, :] + a for n, a in enumerate(products())],
                     b_refs, e_refs, o_ref, slice(0, tm))

    if valid is None:
        compute()
    else:
        pl.when(valid)(compute)

        @pl.when(jnp.logical_not(valid))
        def _():
            o_ref[...] = jnp.zeros(o_ref.shape, o_ref.dtype)


def _linear(x, weights, epilogue, out_cols, out_dtype, *, tm, tn, n_k=1, row_split=1,
            biases=(), extras=(), epi_scratch=(), group=None, name):
    M, K = x.shape
    tk = K // n_k
    assert M % tm == 0 and K % n_k == 0 and out_cols % tn == 0 and tm % row_split == 0
    n_w, n_b = len(weights), len(biases)
    n_pref = len(group) if group is not None else 0
    n_j = out_cols // tn
    grid = (n_j, n_k, M // tm)
    last_k = n_k - 1

    if group is None:
        def row(i, *pref):
            return i
    else:
        def row(i, te, nu, nx):
            return jnp.minimum(i, nu[0] - 1)

    in_specs = [pl.BlockSpec((tm, tk), lambda j, kk, i, *p: (row(i, *p), kk))]
    args = [x]
    for w, _ in weights:
        in_specs.append(pl.BlockSpec(memory_space=pl.ANY))
        args.append(w)
    for b, off in biases:
        in_specs.append(pl.BlockSpec((1, tn), lambda j, kk, i, *p, off=off: (0, off + j)))
        args.append(b)
    for arr, blk, imap in extras:
        in_specs.append(pl.BlockSpec(blk, imap))
        args.append(arr)

    if n_k == 1:
        out_map = lambda j, kk, i, *p: (i, j)
    else:
        assert group is None
        out_map = lambda j, kk, i: (jnp.where(kk == last_k, i, 0), j)
    out_spec = pl.BlockSpec((tm, tn), out_map)

    scratch = [pltpu.VMEM((n_w, tk, tn), f32), pltpu.VMEM((n_w, tk, tn), bf16),
               pltpu.SemaphoreType.DMA((n_w,))]
    vmem = (2 * _nbytes((tm, tk), bf16) + n_w * _nbytes((tk, tn), f32)
            + n_w * _nbytes((tk, tn), bf16) + 2 * _nbytes((tm, tn), out_dtype)
            + 3 * n_w * _nbytes((tm // row_split, tn), f32))
    for arr, blk, _ in extras:
        vmem += 2 * _nbytes([d for d in blk if d is not None], arr.dtype)
    if n_k > 1:
        scratch.append(pltpu.VMEM((n_w, M, tn), f32))
        vmem += n_w * _nbytes((M, tn), f32)
    for shape in epi_scratch:
        scratch.append(pltpu.VMEM(shape, f32))
        vmem += _nbytes(shape, f32)

    body = functools.partial(_mm_body, n_pref=n_pref, n_w=n_w, n_b=n_b, n_x=len(extras),
                             n_k=n_k, n_j=n_j, tk=tk, tn=tn,
                             w_offs=[off for _, off in weights], row_split=row_split,
                             epilogue=epilogue)
    call = pl.pallas_call(
        body,
        out_shape=jax.ShapeDtypeStruct((M, out_cols), out_dtype),
        grid_spec=pltpu.PrefetchScalarGridSpec(
            num_scalar_prefetch=n_pref, grid=grid, in_specs=in_specs, out_specs=out_spec,
            scratch_shapes=scratch),
        compiler_params=_params(3, vmem),
        name=name,
    )
    pref_args = list(group) if group is not None else []
    return call(*pref_args, *args)


def _epi_plain(accs, b_refs, e_refs, o_ref, rs):
    y = accs[0]
    if b_refs:
        y = y + b_refs[0][...]
    o_ref[rs, :] = y.astype(o_ref.dtype)


def _epi_swiglu(accs, b_refs, e_refs, o_ref, rs):
    g, u = accs
    o_ref[rs, :] = (g * jax.nn.sigmoid(g) * u).astype(o_ref.dtype)


def _epi_glu_dwconv(accs, b_refs, e_refs, o_ref, rs, *, tiles_per_seq, rb):
    w_ref, bdw_ref, win_ref, carry_ref = e_refs
    i = pl.program_id(2)
    sub = rs.stop - rs.start
    a = accs[0] + b_refs[0][...]
    g = accs[1] + b_refs[1][...]
    u = a * jax.nn.sigmoid(g)
    halo = carry_ref[...]
    if rs.start == 0:
        halo = jnp.where(i % tiles_per_seq == 0, jnp.zeros_like(halo), halo)
    win_ref[0, 0:HALO, :] = halo
    win_ref[0, HALO:, :] = u
    carry_ref[...] = u[sub - HALO:, :]
    first = HALO - (CONV_WIDTH - 1)
    span = sub + HALO - SUBLANES
    for r in range(1, SUBLANES):
        win_ref[r, 0:span, :] = win_ref[0, r:r + span, :]
    for r0 in range(0, sub, rb):
        acc = jnp.zeros((rb, u.shape[1]), f32) + bdw_ref[...]
        for tap in range(CONV_WIDTH):
            r = (first + tap) % SUBLANES
            base = r0 + first + tap - r
            acc = acc + w_ref[tap:tap + 1, :] * win_ref[r, base:base + rb, :]
        o_ref[rs.start + r0:rs.start + r0 + rb, :] = acc


def _epi_rope(accs, b_refs, e_refs, o_ref, rs, *, rot_half):
    y = accs[0]
    c, su, sd = (e[rs, :] for e in e_refs)
    for g in range(y.shape[1] // LANES):
        yg = y[:, g * LANES:(g + 1) * LANES]
        up = pltpu.roll(yg, rot_half, axis=1)
        dn = pltpu.roll(yg, LANES - rot_half, axis=1)
        o_ref[rs, g * LANES:(g + 1) * LANES] = (yg * c + up * su + dn * sd).astype(o_ref.dtype)


def _adaln_body(c_ref, w_ref, b_ref, o_ref):
    c = c_ref[...]
    sc = (c * jax.nn.sigmoid(c)).astype(bf16)
    y = jnp.dot(sc, w_ref[...].astype(bf16), preferred_element_type=f32)
    o_ref[...] = y + b_ref[...]


def _adaln(c, ada_w, ada_b, *, tn=512):
    B, D = c.shape
    L = ada_w.shape[0] * ada_w.shape[1]
    w = ada_w.reshape(L, D, 3 * D)
    b = ada_b.reshape(L, 1, 3 * D)
    rows = 8
    c_pad = jnp.zeros((rows, D), f32).at[:B].set(c)
    vmem = 2 * _nbytes((D, tn), f32) + _nbytes((D, tn), bf16) + 4 * _nbytes((rows, D), f32)
    out = pl.pallas_call(
        _adaln_body,
        out_shape=jax.ShapeDtypeStruct((L, rows, 3 * D), f32),
        grid=(L, 3 * D // tn),
        in_specs=[pl.BlockSpec((rows, D), lambda m, j: (0, 0)),
                  pl.BlockSpec((None, D, tn), lambda m, j: (m, 0, j)),
                  pl.BlockSpec((None, 1, tn), lambda m, j: (m, 0, j))],
        out_specs=pl.BlockSpec((None, rows, tn), lambda m, j: (m, 0, j)),
        compiler_params=_params(2, vmem),
        name="adaln",
    )(c_pad, w, b)
    return out[:, :B]


def _modulate_body(x_ref, scale_ref, shift_ref, h_ref):
    h_ref[...] = (x_ref[...] * (1.0 + scale_ref[...]) + shift_ref[...]).astype(h_ref.dtype)


def _modulate(x2, scale, shift, seq, *, tm=512):
    T, D = x2.shape
    per_b = seq // tm
    vec = pl.BlockSpec((None, 1, D), lambda i: (i // per_b, 0, 0))
    return pl.pallas_call(
        _modulate_body,
        out_shape=jax.ShapeDtypeStruct((T, D), bf16),
        grid=(T // tm,),
        in_specs=[pl.BlockSpec((tm, D), lambda i: (i, 0)), vec, vec],
        out_specs=pl.BlockSpec((tm, D), lambda i: (i, 0)),
        compiler_params=_params(1, 2 * _nbytes((tm, D), f32) * 2),
        name="modulate",
    )(x2, scale, shift)


def _layer_norm_rows(z, g, b):
    mu = jnp.mean(z, axis=-1, keepdims=True)
    zc = z - mu
    var = jnp.mean(zc * zc, axis=-1, keepdims=True)
    return zc * lax.rsqrt(var + NORM_EPS) * g + b


def _top2_route(logits, n_experts):
    lane_i = lax.broadcasted_iota(jnp.int32, logits.shape, 1)
    lane = lane_i.astype(f32)
    lg = jnp.where(lane_i < n_experts, logits, -jnp.inf)
    m1 = jnp.max(lg, axis=1, keepdims=True)
    i1 = jnp.min(jnp.where(lg == m1, lane, float(LANES)), axis=1, keepdims=True)
    lg2 = jnp.where(lane == i1, -jnp.inf, lg)
    m2 = jnp.max(lg2, axis=1, keepdims=True)
    i2 = jnp.min(jnp.where(lg2 == m2, lane, float(LANES)), axis=1, keepdims=True)
    e = jnp.exp(m2 - m1)
    w1 = 1.0 / (1.0 + e)
    w2 = e / (1.0 + e)
    idx = jnp.where(lane_i == 0, i1, jnp.where(lane_i == 1, i2, 0.0)).astype(jnp.int32)
    wts = jnp.where(lane_i == 0, w1, jnp.where(lane_i == 1, w2, 0.0))
    return idx, wts


def _deepnorm_body(*refs, alpha, mode):
    x_ref, y_ref, gate_ref, g_ref, b_ref = refs[:5]
    z = alpha * x_ref[...] + (1.0 + gate_ref[...]) * y_ref[...]
    xo = _layer_norm_rows(z, g_ref[...], b_ref[...])
    if mode == "last":
        refs[5][...] = xo
        return
    scale_ref, shift_ref = refs[5:7]
    h = xo * (1.0 + scale_ref[...]) + shift_ref[...]
    if mode == "next":
        xo_ref, h_ref = refs[7:9]
        xo_ref[...] = xo
        h_ref[...] = h.astype(h_ref.dtype)
        return
    wr_ref, xo_ref, h_ref, idx_ref, wts_ref = refs[7:12]
    xo_ref[...] = xo
    h_ref[...] = h
    logits = jnp.dot(h, wr_ref[...], preferred_element_type=f32,
                     precision=lax.Precision.HIGHEST)
    idx, wts = _top2_route(logits, N_EXPERTS)
    idx_ref[...] = idx
    wts_ref[...] = wts


def _deepnorm(x2, y2, gate, g, b, seq, alpha, *, nxt=None, router=None, tm=256):
    T, D = x2.shape
    per_b = seq // tm
    rowblk = pl.BlockSpec((tm, D), lambda i: (i, 0))
    vec = pl.BlockSpec((None, 1, D), lambda i: (i // per_b, 0, 0))
    par = pl.BlockSpec((1, D), lambda i: (0, 0))
    lane_blk = pl.BlockSpec((tm, LANES), lambda i: (i, 0))
    in_specs = [rowblk, rowblk, vec, par, par]
    args = [x2, y2, gate, g.reshape(1, D), b.reshape(1, D)]
    out_shape = [jax.ShapeDtypeStruct((T, D), f32)]
    out_specs = [rowblk]
    n_blk = 3
    mode = "last"
    if nxt is not None:
        mode = "next"
        in_specs += [vec, vec]
        args += list(nxt)
        out_shape.append(jax.ShapeDtypeStruct((T, D), bf16 if router is None else f32))
        out_specs.append(rowblk)
        n_blk = 4
    if router is not None:
        mode = "router"
        in_specs.append(pl.BlockSpec((D, LANES), lambda i: (0, 0)))
        args.append(router)
        out_shape += [jax.ShapeDtypeStruct((T, LANES), jnp.int32),
                      jax.ShapeDtypeStruct((T, LANES), f32)]
        out_specs += [lane_blk, lane_blk]
    vmem = (2 * n_blk + 3) * _nbytes((tm, D), f32) + 2 * _nbytes((D, LANES), f32)
    return pl.pallas_call(
        functools.partial(_deepnorm_body, alpha=alpha, mode=mode),
        out_shape=out_shape,
        grid=(T // tm,),
        in_specs=in_specs,
        out_specs=out_specs,
        compiler_params=_params(1, vmem),
        name="deepnorm_" + mode,
    )(*args)


def _rope_table_body(pos_ref, invf_ref, cos_ref, sin_up_ref, sin_dn_ref, *, rot_half):
    ang = pos_ref[...] * invf_ref[...]
    lane = lax.broadcasted_iota(jnp.int32, ang.shape, 1)
    c, s = jnp.cos(ang), jnp.sin(ang)
    cos_ref[0] = jnp.where(lane < 2 * rot_half, c, 1.0)
    sin_up_ref[0] = jnp.where((lane >= rot_half) & (lane < 2 * rot_half), s, 0.0)
    sin_dn_ref[0] = jnp.where(lane < rot_half, -s, 0.0)
    cos_ref[1] = jnp.ones(ang.shape, f32)
    sin_up_ref[1] = jnp.zeros(ang.shape, f32)
    sin_dn_ref[1] = jnp.zeros(ang.shape, f32)


def _rope_tables(positions, rot_dim, *, tm=1024):
    T = positions.size
    half = rot_dim // 2
    inv_freq = 1.0 / (ROPE_THETA ** (jnp.arange(0, rot_dim, 2, dtype=f32) / rot_dim))
    invf = jnp.zeros((1, LANES), f32).at[0, :rot_dim].set(jnp.tile(inv_freq, 2))
    pos = positions.astype(f32).reshape(T, 1)
    tab = pl.BlockSpec((2, tm, LANES), lambda i: (0, i, 0))
    return pl.pallas_call(
        functools.partial(_rope_table_body, rot_half=half),
        out_shape=[jax.ShapeDtypeStruct((2, T, LANES), f32)] * 3,
        grid=(T // tm,),
        in_specs=[pl.BlockSpec((tm, 1), lambda i: (i, 0)),
                  pl.BlockSpec((1, LANES), lambda i: (0, 0))],
        out_specs=[tab, tab, tab],
        compiler_params=_params(1, 12 * _nbytes((tm, LANES), f32)),
        name="rope_tables",
    )(pos, invf)


def _diff_attn_body(q_ref, k_ref, v_ref, lam_ref, g_ref, o_ref, acc_ref, m_ref, l_ref,
                    *, tq, tk, scale, lambda_init):
    qi = pl.program_id(2)
    dh = DA_HEAD_DIM
    c2 = scale * math.log2(math.e)
    k_reps = tk // LANES
    v_reps = 2 * dh // LANES
    n_diag = tq // tk
    m_ref[...] = jnp.full(m_ref.shape, NEG_BIG, f32)
    l_ref[...] = jnp.zeros(l_ref.shape, f32)
    acc_ref[...] = jnp.zeros(acc_ref.shape, f32)

    def step(ki, diag):
        rows = pl.ds(pl.multiple_of(ki * tk, tk), tk)
        v = v_ref[rows, :]
        for c in range(2):
            s = lax.dot_general(q_ref[:, c * dh:(c + 1) * dh], k_ref[rows, c * dh:(c + 1) * dh],
                                (((1,), (1,)), ((), ())), preferred_element_type=f32)
            if diag is not None:
                r = lax.broadcasted_iota(jnp.int32, s.shape, 0)
                col = lax.broadcasted_iota(jnp.int32, s.shape, 1)
                s = jnp.where(col + diag * tk <= r, s, NEG_BIG)
            m_old = m_ref[c]
            m_new = jnp.maximum(m_old, jnp.max(s, axis=1, keepdims=True))
            a = jnp.exp2((m_old - m_new) * c2)
            p = jnp.exp2((s - jnp.tile(m_new, (1, k_reps))) * c2)
            l_ref[c] = a * l_ref[c] + jnp.sum(p, axis=1, keepdims=True)
            acc_ref[c] = (jnp.tile(a, (1, v_reps)) * acc_ref[c]
                          + jnp.dot(p.astype(bf16), v, preferred_element_type=f32))
            m_ref[c] = m_new

    n_below = qi * n_diag

    def below_pair(pi, carry):
        step(2 * pi, None)
        step(2 * pi + 1, None)
        return carry

    lax.fori_loop(0, n_below // 2, below_pair, 0)

    @pl.when(n_below % 2 == 1)
    def _():
        step(n_below - 1, None)
        step(n_below, 0)

    @pl.when(n_below % 2 == 0)
    def _():
        step(n_below, 0)

    for d in range(1, n_diag):
        step(n_below + d, d)

    lv = lam_ref[...]
    lam = (jnp.exp(jnp.sum(lv[0:1] * lv[1:2], axis=1, keepdims=True))
           - jnp.exp(jnp.sum(lv[2:3] * lv[3:4], axis=1, keepdims=True)) + lambda_init)
    o = (acc_ref[0] / jnp.tile(l_ref[0], (1, v_reps))
         - lam * (acc_ref[1] / jnp.tile(l_ref[1], (1, v_reps))))
    y = o * lax.rsqrt(jnp.mean(o * o, axis=1, keepdims=True) + NORM_EPS)
    o_ref[...] = (y * g_ref[...] * (1.0 - lambda_init)).astype(o_ref.dtype)


def _diff_attention(qkv, lam_vecs, subln_g, batch, seq, heads, lambda_init, *, tq=512, tk=512):
    T = qkv.shape[0]
    hw = 2 * DA_HEAD_DIM
    D = heads * hw
    nq = seq // tq
    vmem = (4 * _nbytes((seq, hw), bf16) + 4 * _nbytes((tq, hw), bf16)
            + 2 * _nbytes((tq, hw), f32) + 4 * _nbytes((tq, LANES), f32)
            + 8 * _nbytes((tq, tk), f32))
    return pl.pallas_call(
        functools.partial(_diff_attn_body, tq=tq, tk=tk, scale=DA_HEAD_DIM ** -0.5,
                          lambda_init=lambda_init),
        out_shape=jax.ShapeDtypeStruct((T, D), bf16),
        grid=(batch, heads, nq),
        in_specs=[pl.BlockSpec((tq, hw), lambda b, h, qi: (b * nq + qi, h)),
                  pl.BlockSpec((seq, hw), lambda b, h, qi: (b, heads + h)),
                  pl.BlockSpec((seq, hw), lambda b, h, qi: (b, 2 * heads + h)),
                  pl.BlockSpec((4, DA_HEAD_DIM), lambda b, h, qi: (0, 0)),
                  pl.BlockSpec((1, hw), lambda b, h, qi: (0, 0))],
        out_specs=pl.BlockSpec((tq, hw), lambda b, h, qi: (b * nq + qi, h)),
        scratch_shapes=[pltpu.VMEM((2, tq, hw), f32), pltpu.VMEM((2, tq, LANES), f32),
                        pltpu.VMEM((2, tq, LANES), f32)],
        compiler_params=_params(3, vmem),
        name="diff_attention",
    )(qkv, qkv, qkv, lam_vecs, subln_g.reshape(1, hw))


HALO = 32


def _ln_swish_body(v_ref, g_ref, b_ref, o_ref):
    y = _layer_norm_rows(v_ref[...], g_ref[...], b_ref[...])
    o_ref[...] = (y * jax.nn.sigmoid(y)).astype(o_ref.dtype)


def _ln_swish(v, ln_g, ln_b, *, tm=256):
    T, D = v.shape
    par = pl.BlockSpec((1, D), lambda i: (0, 0))
    return pl.pallas_call(
        _ln_swish_body,
        out_shape=jax.ShapeDtypeStruct((T, D), bf16),
        grid=(T // tm,),
        in_specs=[pl.BlockSpec((tm, D), lambda i: (i, 0)), par, par],
        out_specs=pl.BlockSpec((tm, D), lambda i: (i, 0)),
        compiler_params=_params(1, 6 * _nbytes((tm, D), f32)),
        name="ln_swish",
    )(v, ln_g.reshape(1, D), ln_b.reshape(1, D))


DMA_ISSUE_UNROLL = 8


def _gather_rows_body(idx_ref, src_ref, o_ref, buf_ref, sem, *, tg, n_steps):
    i = pl.program_id(0)

    def row_copy(step, r, slot):
        return pltpu.make_async_copy(
            src_ref.at[pl.ds(idx_ref[step * tg + r], 1), :],
            buf_ref.at[slot, pl.ds(r, 1), :], sem.at[slot])

    def start_tile(step):
        def start(r, carry):
            row_copy(step, r, step % 2).start()
            return carry
        lax.fori_loop(0, tg, start, 0, unroll=DMA_ISSUE_UNROLL)

    @pl.when(i == 0)
    def _():
        start_tile(i)

    @pl.when(i + 1 < n_steps)
    def _():
        start_tile(i + 1)

    def wait(r, carry):
        row_copy(i, r, i % 2).wait()
        return carry

    lax.fori_loop(0, tg, wait, 0, unroll=DMA_ISSUE_UNROLL)
    o_ref[...] = buf_ref[i % 2].astype(o_ref.dtype)


def _gather_rows(src, idx, out_dtype, *, tg=256):
    P = idx.shape[0]
    D = src.shape[1]
    n_steps = P // tg
    return pl.pallas_call(
        functools.partial(_gather_rows_body, tg=tg, n_steps=n_steps),
        out_shape=jax.ShapeDtypeStruct((P, D), out_dtype),
        grid_spec=pltpu.PrefetchScalarGridSpec(
            num_scalar_prefetch=1, grid=(n_steps,),
            in_specs=[pl.BlockSpec(memory_space=pl.ANY)],
            out_specs=pl.BlockSpec((tg, D), lambda i, idx: (i, 0)),
            scratch_shapes=[pltpu.VMEM((2, tg, D), f32), pltpu.SemaphoreType.DMA((2,))]),
        compiler_params=_params(1, 6 * _nbytes((tg, D), f32)),
        name="gather_rows",
    )(idx, src)


def _moe_combine_body(pos_ref, x_ref, ye_ref, wts_ref, gate_ref, g_ref, b_ref, xo_ref,
                      buf_ref, sem, *, tm, n_steps, alpha):
    i = pl.program_id(0)

    def row_copy(step, r, k, slot):
        return pltpu.make_async_copy(
            ye_ref.at[pl.ds(pos_ref[(step * tm + r) * TOP_K + k], 1), :],
            buf_ref.at[slot, k, pl.ds(r, 1), :], sem.at[slot])

    def start_tile(step):
        def start(r, carry):
            for k in range(TOP_K):
                row_copy(step, r, k, step % 2).start()
            return carry
        lax.fori_loop(0, tm, start, 0, unroll=DMA_ISSUE_UNROLL)

    @pl.when(i == 0)
    def _():
        start_tile(i)

    @pl.when(i + 1 < n_steps)
    def _():
        start_tile(i + 1)

    def wait(r, carry):
        for k in range(TOP_K):
            row_copy(i, r, k, i % 2).wait()
        return carry

    lax.fori_loop(0, tm, wait, 0, unroll=DMA_ISSUE_UNROLL)
    wts = wts_ref[...]
    y = wts[:, 0:1] * buf_ref[i % 2, 0] + wts[:, 1:2] * buf_ref[i % 2, 1]
    z = alpha * x_ref[...] + (1.0 + gate_ref[...]) * y
    xo_ref[...] = _layer_norm_rows(z, g_ref[...], b_ref[...])


def _moe_combine_deepnorm(x2, ye, pos, wts, gate, g, b, seq, alpha, *, tm=256):
    T, D = x2.shape
    per_b = seq // tm
    return pl.pallas_call(
        functools.partial(_moe_combine_body, tm=tm, n_steps=T // tm, alpha=alpha),
        out_shape=jax.ShapeDtypeStruct((T, D), f32),
        grid_spec=pltpu.PrefetchScalarGridSpec(
            num_scalar_prefetch=1, grid=(T // tm,),
            in_specs=[pl.BlockSpec((tm, D), lambda i, p: (i, 0)),
                      pl.BlockSpec(memory_space=pl.ANY),
                      pl.BlockSpec((tm, LANES), lambda i, p: (i, 0)),
                      pl.BlockSpec((None, 1, D), lambda i, p: (i // per_b, 0, 0)),
                      pl.BlockSpec((1, D), lambda i, p: (0, 0)),
                      pl.BlockSpec((1, D), lambda i, p: (0, 0))],
            out_specs=pl.BlockSpec((tm, D), lambda i, p: (i, 0)),
            scratch_shapes=[pltpu.VMEM((2, TOP_K, tm, D), f32),
                            pltpu.SemaphoreType.DMA((2,))]),
        compiler_params=_params(1, 11 * _nbytes((tm, D), f32)),
        name="moe_combine_deepnorm",
    )(pos, x2, ye, wts, gate, g.reshape(1, D), b.reshape(1, D))


def _route_tables(top_i, tm):
    n_assign = top_i.size
    n_tok = top_i.shape[0]
    P = n_assign + N_EXPERTS * tm
    n_tiles = P // tm
    flat_e = top_i.reshape(-1)
    onehot = (flat_e[:, None] == jnp.arange(N_EXPERTS, dtype=jnp.int32)[None, :]).astype(jnp.int32)
    csum = jnp.cumsum(onehot, axis=0)
    rank = jnp.sum(onehot * (csum - 1), axis=1)
    counts = csum[-1]
    tiles_e = (counts + tm - 1) // tm
    tile_end = jnp.cumsum(tiles_e)
    starts = (tile_end - tiles_e) * tm
    pos = (starts[flat_e] + rank).astype(jnp.int32)
    token = jnp.arange(n_assign, dtype=jnp.int32) // (n_assign // n_tok)
    row_token = jnp.zeros((P,), jnp.int32).at[pos].set(token)
    n_used = tile_end[-1].astype(jnp.int32)
    tile_ids = jnp.minimum(jnp.arange(n_tiles, dtype=jnp.int32), n_used - 1)
    tile_expert = jnp.minimum(
        jnp.sum((tile_ids[:, None] >= tile_end[None, :]).astype(jnp.int32), axis=1),
        N_EXPERTS - 1).astype(jnp.int32)
    experts = jnp.arange(N_EXPERTS, dtype=jnp.int32)
    later_nonempty = (experts[None, :] > experts[:, None]) & (tiles_e[None, :] > 0)
    next_e = jnp.min(jnp.where(later_nonempty, experts[None, :], N_EXPERTS), axis=1)
    next_e = jnp.where(next_e < N_EXPERTS, next_e, -1).astype(jnp.int32)
    return pos, row_token, (tile_expert, n_used.reshape(1), next_e[tile_expert])


def kernel(x, c, positions, ada_w, ada_b, ln_g, ln_b, attn_w_qkv, attn_lambda, attn_subln_g, attn_w_o, conv_w_in, conv_b_in, conv_w_dw, conv_b_dw, conv_ln_g, conv_ln_b, conv_w_out, conv_b_out, ffn_w_in, ffn_w_out, moe_w_router, moe_w_in, moe_w_out):
    B, S, D = x.shape
    T = B * S
    depth = ada_w.shape[0]
    alpha = (2 * depth) ** 0.25
    heads = D // (2 * DA_HEAD_DIM)
    rot_dim = DA_HEAD_DIM // 4
    d_ff = ffn_w_out.shape[1]
    d_ffe = moe_w_out.shape[2]

    mods = _adaln(c, ada_w, ada_b)

    def mod(i, s):
        m = mods[2 * i + s]
        return [m[:, k * D:(k + 1) * D].reshape(B, 1, D) for k in range(3)]

    cos_t, sin_up, sin_dn = _rope_tables(positions, rot_dim)
    x2 = x.reshape(T, D)
    shift, scale, gate = mod(0, 0)
    h = _modulate(x2, scale, shift, S)

    for i in range(depth):
        jm = i // N_MIXERS
        if i % N_MIXERS == 0:
            lambda_init = 0.8 - 0.6 * math.exp(-0.3 * i)
            tn = 1024
            n_rope_blocks = 2 * D // tn
            tab = lambda j, kk, r: (jnp.where(j < n_rope_blocks, 0, 1), r, 0)
            qkv = _linear(
                h, [(attn_w_qkv[jm], 0)], functools.partial(_epi_rope, rot_half=rot_dim // 2),
                3 * D, bf16, tm=1024, tn=tn, row_split=4, name="qkv_rope",
                extras=[(t, (None, 1024, LANES), tab) for t in (cos_t, sin_up, sin_dn)])
            o = _diff_attention(qkv, attn_lambda[jm], attn_subln_g[jm], B, S, heads, lambda_init)
            y = _linear(o, [(attn_w_o[jm], 0)], _epi_plain, D, f32, tm=1024, tn=512,
                        name="attn_out")
        else:
            tm_c, tn_c, split = 1024, 512, 8
            nb = D // tn_c
            chan = lambda j, kk, r: (0, j)
            cv = _linear(h, [(conv_w_in[jm], 0), (conv_w_in[jm], nb)],
                         functools.partial(_epi_glu_dwconv, tiles_per_seq=S // tm_c, rb=64),
                         D, f32, tm=tm_c, tn=tn_c, row_split=split, name="conv_in_glu_dwconv",
                         biases=[(conv_b_in[jm].reshape(1, 2 * D), 0),
                                 (conv_b_in[jm].reshape(1, 2 * D), nb)],
                         extras=[(conv_w_dw[jm], (CONV_WIDTH, tn_c), chan),
                                 (conv_b_dw[jm].reshape(1, D), (1, tn_c), chan)],
                         epi_scratch=[(SUBLANES, tm_c // split + HALO, tn_c), (HALO, tn_c)])
            a = _ln_swish(cv, conv_ln_g[jm], conv_ln_b[jm])
            y = _linear(a, [(conv_w_out[jm], 0)], _epi_plain, D, f32, tm=1024, tn=512,
                        name="conv_out", biases=[(conv_b_out[jm].reshape(1, D), 0)])

        shift, scale, _ = mod(i, 1)
        jf = i // FFN_PERIOD
        is_moe = i % FFN_PERIOD != 0
        if not is_moe:
            x2, h = _deepnorm(x2, y, gate, ln_g[i, 0], ln_b[i, 0], S, alpha, nxt=(scale, shift))
        else:
            w_router = jnp.zeros((D, LANES), f32).at[:, :N_EXPERTS].set(moe_w_router[jf])
            x2, h32, idx, wts = _deepnorm(x2, y, gate, ln_g[i, 0], ln_b[i, 0], S, alpha,
                                          nxt=(scale, shift), router=w_router)
        _, _, gate = mod(i, 1)

        if not is_moe:
            nb = d_ff // 512
            act = _linear(h, [(ffn_w_in[jf], 0), (ffn_w_in[jf], nb)], _epi_swiglu, d_ff, bf16,
                          tm=1024, tn=512, row_split=2, name="ffn_in_swiglu")
            y = _linear(act, [(ffn_w_out[jf], 0)], _epi_plain, D, f32, tm=1024, tn=512,
                        n_k=4, name="ffn_out")
        else:
            tm_e = 256
            pos, row_token, groups = _route_tables(idx[:, :TOP_K], tm_e)
            hs = _gather_rows(h32, row_token, bf16)
            nb = d_ffe // 512
            act = _linear(hs, [(moe_w_in[jf], 0), (moe_w_in[jf], nb)], _epi_swiglu, d_ffe, bf16,
                          tm=tm_e, tn=512, name="moe_in_swiglu", group=groups)
            ye = _linear(act, [(moe_w_out[jf], 0)], _epi_plain, D, f32, tm=tm_e, tn=1024,
                         name="moe_out", group=groups)

        last = i == depth - 1
        if is_moe:
            x2 = _moe_combine_deepnorm(x2, ye, pos, wts, gate, ln_g[i, 1], ln_b[i, 1], S, alpha)
            if not last:
                shift, scale, gate = mod(i + 1, 0)
                h = _modulate(x2, scale, shift, S)
        elif last:
            (x2,) = _deepnorm(x2, y, gate, ln_g[i, 1], ln_b[i, 1], S, alpha)
        else:
            shift, scale, gate_next = mod(i + 1, 0)
            x2, h = _deepnorm(x2, y, gate, ln_g[i, 1], ln_b[i, 1], S, alpha, nxt=(scale, shift))
            gate = gate_next
    return x2.reshape(B, S, D)
```

```python
import functools
import math

import jax
import jax.numpy as jnp
from jax import lax
from jax.experimental import pallas as pl
from jax.experimental.pallas import tpu as pltpu

DA_HEAD_DIM = 128
ROPE_THETA = 500000.0
CONV_WIDTH = 31
N_EXPERTS = 8
TOP_K = 2
NORM_EPS = 1e-5
N_MIXERS = 2
FFN_PERIOD = 2

LANES = 128
SUBLANES = 8
V7X_VMEM_BYTES = 64 * 1024 * 1024
VMEM_REQUEST_CAP = V7X_VMEM_BYTES - 3 * 1024 * 1024
NEG_BIG = -1e30
N_DMA_PRIORITIES = 2
WEIGHT_DMA_PRIORITY = 1

f32 = jnp.float32
bf16 = jnp.bfloat16


def _params(n_grid, vmem_bytes):
    limit = min(int(vmem_bytes * 1.2) + (2 << 20), VMEM_REQUEST_CAP)
    return pltpu.CompilerParams(
        dimension_semantics=("arbitrary",) * n_grid, vmem_limit_bytes=limit)


def _nbytes(shape, dtype):
    return math.prod(shape) * jnp.dtype(dtype).itemsize


def _mm_body(*refs, n_pref, n_w, n_b, n_x, n_k, n_j, tk, tn, w_offs, row_split, epilogue):
    pref = refs[:n_pref]
    x_ref = refs[n_pref]
    w_refs = refs[n_pref + 1:n_pref + 1 + n_w]
    b_refs = refs[n_pref + 1 + n_w:n_pref + 1 + n_w + n_b]
    e_refs = refs[n_pref + 1 + n_w + n_b:n_pref + 1 + n_w + n_b + n_x]
    o_ref = refs[n_pref + 1 + n_w + n_b + n_x]
    scratch = refs[n_pref + 2 + n_w + n_b + n_x:]
    stage_ref, wbf_ref, sem = scratch[:3]
    acc_ref = scratch[3] if n_k > 1 else None
    e_refs = tuple(e_refs) + tuple(scratch[4 if n_k > 1 else 3:])
    j, kk, i = pl.program_id(0), pl.program_id(1), pl.program_id(2)
    tm = x_ref.shape[0]

    if n_pref:
        te_ref, nu_ref, nx_ref = pref
        e_cur = te_ref[i]
        fresh = (i == 0) | (e_cur != te_ref[jnp.maximum(i - 1, 0)])
        valid = i < nu_ref[0]
        within = nx_ref[i] >= 0
        e_next = jnp.where(within, nx_ref[i], te_ref[0])
    else:
        e_cur = e_next = None
        fresh = i == 0
        valid = None
        within = jnp.bool_(False)
    more_k = kk + 1 < n_k
    stay = jnp.logical_or(within, more_k)
    kk_next = jnp.where(within, kk, jnp.where(more_k, kk + 1, 0))
    j_next = jnp.where(stay, j, j + 1)
    has_next = jnp.logical_or(stay, j + 1 < n_j)

    def w_copy(n, e, kk_, j_):
        src = w_refs[n] if e is None else w_refs[n].at[e]
        src = src.at[pl.ds(pl.multiple_of(kk_ * tk, tk), tk),
                     pl.ds(pl.multiple_of((w_offs[n] + j_) * tn, tn), tn)]
        return pltpu.make_async_copy(src, stage_ref.at[n], sem.at[n])

    @pl.when(fresh)
    def _():
        @pl.when((j == 0) & (kk == 0) & (i == 0))
        def _():
            for n in range(n_w):
                w_copy(n, e_cur, kk, j).start(priority=WEIGHT_DMA_PRIORITY)

        for n in range(n_w):
            w_copy(n, e_cur, kk, j).wait()
            wbf_ref[n] = stage_ref[n].astype(bf16)

        @pl.when(has_next)
        def _():
            for n in range(n_w):
                w_copy(n, e_next, kk_next, j_next).start(priority=WEIGHT_DMA_PRIORITY)

    def compute():
        if n_k == 1:
            sub = tm // row_split
            for s in range(row_split):
                rs = slice(s * sub, (s + 1) * sub)
                x = x_ref[rs, :]
                accs = [jnp.dot(x, wbf_ref[n], preferred_element_type=f32) for n in range(n_w)]
                epilogue(accs, b_refs, e_refs, o_ref, rs)
            return
        rows = pl.ds(pl.multiple_of(i * tm, tm), tm)

        def products():
            x = x_ref[...]
            return [jnp.dot(x, wbf_ref[n], preferred_element_type=f32) for n in range(n_w)]

        @pl.when(kk == 0)
        def _():
            for n, a in enumerate(products()):
                acc_ref[n, rows, :] = a

        @pl.when((kk > 0) & (kk < n_k - 1))
        def _():
            for n, a in enumerate(products()):
                acc_ref[n, rows, :] += a

        @pl.when(kk == n_k - 1)
        def _():
            epilogue([acc_ref[n, rows, :] + a for n, a in enumerate(products())],
                     b_refs, e_refs, o_ref, slice(0, tm))

    if valid is None:
        compute()
    else:
        pl.when(valid)(compute)

        @pl.when(jnp.logical_not(valid))
        def _():
            o_ref[...] = jnp.zeros(o_ref.shape, o_ref.dtype)


def _linear(x, weights, epilogue, out_cols, out_dtype, *, tm, tn, n_k=1, row_split=1,
            biases=(), extras=(), epi_scratch=(), group=None, name):
    M, K = x.shape
    tk = K // n_k
    assert M % tm == 0 and K % n_k == 0 and out_cols % tn == 0 and tm % row_split == 0
    n_w, n_b = len(weights), len(biases)
    n_pref = len(group) if group is not None else 0
    n_j = out_cols // tn
    grid = (n_j, n_k, M // tm)
    last_k = n_k - 1

    if group is None:
        def row(i, *pref):
            return i
    else:
        def row(i, te, nu, nx):
            return jnp.minimum(i, nu[0] - 1)

    in_specs = [pl.BlockSpec((tm, tk), lambda j, kk, i, *p: (row(i, *p), kk))]
    args = [x]
    for w, _ in weights:
        in_specs.append(pl.BlockSpec(memory_space=pl.ANY))
        args.append(w)
    for b, off in biases:
        in_specs.append(pl.BlockSpec((1, tn), lambda j, kk, i, *p, off=off: (0, off + j)))
        args.append(b)
    for arr, blk, imap in extras:
        in_specs.append(pl.BlockSpec(blk, imap))
        args.append(arr)

    if n_k == 1:
        out_map = lambda j, kk, i, *p: (i, j)
    else:
        assert group is None
        out_map = lambda j, kk, i: (jnp.where(kk == last_k, i, 0), j)
    out_spec = pl.BlockSpec((tm, tn), out_map)

    scratch = [pltpu.VMEM((n_w, tk, tn), f32), pltpu.VMEM((n_w, tk, tn), bf16),
               pltpu.SemaphoreType.DMA((n_w,))]
    vmem = (2 * _nbytes((tm, tk), bf16) + n_w * _nbytes((tk, tn), f32)
            + n_w * _nbytes((tk, tn), bf16) + 2 * _nbytes((tm, tn), out_dtype)
            + 3 * n_w * _nbytes((tm // row_split, tn), f32))
    for arr, blk, _ in extras:
        vmem += 2 * _nbytes([d for d in blk if d is not None], arr.dtype)
    if n_k > 1:
        scratch.append(pltpu.VMEM((n_w, M, tn), f32))
        vmem += n_w * _nbytes((M, tn), f32)
    for shape in epi_scratch:
        scratch.append(pltpu.VMEM(shape, f32))
        vmem += _nbytes(shape, f32)

    body = functools.partial(_mm_body, n_pref=n_pref, n_w=n_w, n_b=n_b, n_x=len(extras),
                             n_k=n_k, n_j=n_j, tk=tk, tn=tn,
                             w_offs=[off for _, off in weights], row_split=row_split,
                             epilogue=epilogue)
    call = pl.pallas_call(
        body,
        out_shape=jax.ShapeDtypeStruct((M, out_cols), out_dtype),
        grid_spec=pltpu.PrefetchScalarGridSpec(
            num_scalar_prefetch=n_pref, grid=grid, in_specs=in_specs, out_specs=out_spec,
            scratch_shapes=scratch),
        compiler_params=_params(3, vmem),
        name=name,
    )
    pref_args = list(group) if group is not None else []
    return call(*pref_args, *args)


def _epi_plain(accs, b_refs, e_refs, o_ref, rs):
    y = accs[0]
    if b_refs:
        y = y + b_refs[0][...]
    o_ref[rs, :] = y.astype(o_ref.dtype)


def _epi_swiglu(accs, b_refs, e_refs, o_ref, rs):
    g, u = accs
    o_ref[rs, :] = (g * jax.nn.sigmoid(g) * u).astype(o_ref.dtype)


def _epi_glu_dwconv(accs, b_refs, e_refs, o_ref, rs, *, tiles_per_seq, rb):
    w_ref, bdw_ref, win_ref, carry_ref = e_refs
    i = pl.program_id(2)
    sub = rs.stop - rs.start
    a = accs[0] + b_refs[0][...]
    g = accs[1] + b_refs[1][...]
    u = a * jax.nn.sigmoid(g)
    halo = carry_ref[...]
    if rs.start == 0:
        halo = jnp.where(i % tiles_per_seq == 0, jnp.zeros_like(halo), halo)
    win_ref[0, 0:HALO, :] = halo
    win_ref[0, HALO:, :] = u
    carry_ref[...] = u[sub - HALO:, :]
    first = HALO - (CONV_WIDTH - 1)
    span = sub + HALO - SUBLANES
    for r in range(1, SUBLANES):
        win_ref[r, 0:span, :] = win_ref[0, r:r + span, :]
    for r0 in range(0, sub, rb):
        acc = jnp.zeros((rb, u.shape[1]), f32) + bdw_ref[...]
        for tap in range(CONV_WIDTH):
            r = (first + tap) % SUBLANES
            base = r0 + first + tap - r
            acc = acc + w_ref[tap:tap + 1, :] * win_ref[r, base:base + rb, :]
        o_ref[rs.start + r0:rs.start + r0 + rb, :] = acc


def _epi_rope(accs, b_refs, e_refs, o_ref, rs, *, rot_half):
    y = accs[0]
    c, su, sd = (e[rs, :] for e in e_refs)
    for g in range(y.shape[1] // LANES):
        yg = y[:, g * LANES:(g + 1) * LANES]
        up = pltpu.roll(yg, rot_half, axis=1)
        dn = pltpu.roll(yg, LANES - rot_half, axis=1)
        o_ref[rs, g * LANES:(g + 1) * LANES] = (yg * c + up * su + dn * sd).astype(o_ref.dtype)


def _adaln_body(c_ref, w_ref, b_ref, o_ref):
    c = c_ref[...]
    sc = (c * jax.nn.sigmoid(c)).astype(bf16)
    y = jnp.dot(sc, w_ref[...].astype(bf16), preferred_element_type=f32)
    o_ref[...] = y + b_ref[...]


def _adaln(c, ada_w, ada_b, *, tn=512):
    B, D = c.shape
    L = ada_w.shape[0] * ada_w.shape[1]
    w = ada_w.reshape(L, D, 3 * D)
    b = ada_b.reshape(L, 1, 3 * D)
    rows = 8
    c_pad = jnp.zeros((rows, D), f32).at[:B].set(c)
    vmem = 2 * _nbytes((D, tn), f32) + _nbytes((D, tn), bf16) + 4 * _nbytes((rows, D), f32)
    out = pl.pallas_call(
        _adaln_body,
        out_shape=jax.ShapeDtypeStruct((L, rows, 3 * D), f32),
        grid=(L, 3 * D // tn),
        in_specs=[pl.BlockSpec((rows, D), lambda m, j: (0, 0)),
                  pl.BlockSpec((None, D, tn), lambda m, j: (m, 0, j)),
                  pl.BlockSpec((None, 1, tn), lambda m, j: (m, 0, j))],
        out_specs=pl.BlockSpec((None, rows, tn), lambda m, j: (m, 0, j)),
        compiler_params=_params(2, vmem),
        name="adaln",
    )(c_pad, w, b)
    return out[:, :B]


def _modulate_body(x_ref, scale_ref, shift_ref, h_ref):
    h_ref[...] = (x_ref[...] * (1.0 + scale_ref[...]) + shift_ref[...]).astype(h_ref.dtype)


def _modulate(x2, scale, shift, seq, *, tm=512):
    T, D = x2.shape
    per_b = seq // tm
    vec = pl.BlockSpec((None, 1, D), lambda i: (i // per_b, 0, 0))
    return pl.pallas_call(
        _modulate_body,
        out_shape=jax.ShapeDtypeStruct((T, D), bf16),
        grid=(T // tm,),
        in_specs=[pl.BlockSpec((tm, D), lambda i: (i, 0)), vec, vec],
        out_specs=pl.BlockSpec((tm, D), lambda i: (i, 0)),
        compiler_params=_params(1, 2 * _nbytes((tm, D), f32) * 2),
        name="modulate",
    )(x2, scale, shift)


def _layer_norm_rows(z, g, b):
    mu = jnp.mean(z, axis=-1, keepdims=True)
    zc = z - mu
    var = jnp.mean(zc * zc, axis=-1, keepdims=True)
    return zc * lax.rsqrt(var + NORM_EPS) * g + b


def _top2_route(logits, n_experts):
    lane_i = lax.broadcasted_iota(jnp.int32, logits.shape, 1)
    lane = lane_i.astype(f32)
    lg = jnp.where(lane_i < n_experts, logits, -jnp.inf)
    m1 = jnp.max(lg, axis=1, keepdims=True)
    i1 = jnp.min(jnp.where(lg == m1, lane, float(LANES)), axis=1, keepdims=True)
    lg2 = jnp.where(lane == i1, -jnp.inf, lg)
    m2 = jnp.max(lg2, axis=1, keepdims=True)
    i2 = jnp.min(jnp.where(lg2 == m2, lane, float(LANES)), axis=1, keepdims=True)
    e = jnp.exp(m2 - m1)
    w1 = 1.0 / (1.0 + e)
    w2 = e / (1.0 + e)
    idx = jnp.where(lane_i == 0, i1, jnp.where(lane_i == 1, i2, 0.0)).astype(jnp.int32)
    wts = jnp.where(lane_i == 0, w1, jnp.where(lane_i == 1, w2, 0.0))
    return idx, wts


def _deepnorm_body(*refs, alpha, mode):
    x_ref, y_ref, gate_ref, g_ref, b_ref = refs[:5]
    z = alpha * x_ref[...] + (1.0 + gate_ref[...]) * y_ref[...]
    xo = _layer_norm_rows(z, g_ref[...], b_ref[...])
    if mode == "last":
        refs[5][...] = xo
        return
    scale_ref, shift_ref = refs[5:7]
    h = xo * (1.0 + scale_ref[...]) + shift_ref[...]
    if mode == "next":
        xo_ref, h_ref = refs[7:9]
        xo_ref[...] = xo
        h_ref[...] = h.astype(h_ref.dtype)
        return
    wr_ref, xo_ref, h_ref, idx_ref, wts_ref = refs[7:12]
    xo_ref[...] = xo
    h_ref[...] = h
    logits = jnp.dot(h, wr_ref[...], preferred_element_type=f32,
                     precision=lax.Precision.HIGHEST)
    idx, wts = _top2_route(logits, N_EXPERTS)
    idx_ref[...] = idx
    wts_ref[...] = wts


def _deepnorm(x2, y2, gate, g, b, seq, alpha, *, nxt=None, router=None, tm=256):
    T, D = x2.shape
    per_b = seq // tm
    rowblk = pl.BlockSpec((tm, D), lambda i: (i, 0))
    vec = pl.BlockSpec((None, 1, D), lambda i: (i // per_b, 0, 0))
    par = pl.BlockSpec((1, D), lambda i: (0, 0))
    lane_blk = pl.BlockSpec((tm, LANES), lambda i: (i, 0))
    in_specs = [rowblk, rowblk, vec, par, par]
    args = [x2, y2, gate, g.reshape(1, D), b.reshape(1, D)]
    out_shape = [jax.ShapeDtypeStruct((T, D), f32)]
    out_specs = [rowblk]
    n_blk = 3
    mode = "last"
    if nxt is not None:
        mode = "next"
        in_specs += [vec, vec]
        args += list(nxt)
        out_shape.append(jax.ShapeDtypeStruct((T, D), bf16 if router is None else f32))
        out_specs.append(rowblk)
        n_blk = 4
    if router is not None:
        mode = "router"
        in_specs.append(pl.BlockSpec((D, LANES), lambda i: (0, 0)))
        args.append(router)
        out_shape += [jax.ShapeDtypeStruct((T, LANES), jnp.int32),
                      jax.ShapeDtypeStruct((T, LANES), f32)]
        out_specs += [lane_blk, lane_blk]
    vmem = (2 * n_blk + 3) * _nbytes((tm, D), f32) + 2 * _nbytes((D, LANES), f32)
    return pl.pallas_call(
        functools.partial(_deepnorm_body, alpha=alpha, mode=mode),
        out_shape=out_shape,
        grid=(T // tm,),
        in_specs=in_specs,
        out_specs=out_specs,
        compiler_params=_params(1, vmem),
        name="deepnorm_" + mode,
    )(*args)


def _rope_table_body(pos_ref, invf_ref, cos_ref, sin_up_ref, sin_dn_ref, *, rot_half):
    ang = pos_ref[...] * invf_ref[...]
    lane = lax.broadcasted_iota(jnp.int32, ang.shape, 1)
    c, s = jnp.cos(ang), jnp.sin(ang)
    cos_ref[0] = jnp.where(lane < 2 * rot_half, c, 1.0)
    sin_up_ref[0] = jnp.where((lane >= rot_half) & (lane < 2 * rot_half), s, 0.0)
    sin_dn_ref[0] = jnp.where(lane < rot_half, -s, 0.0)
    cos_ref[1] = jnp.ones(ang.shape, f32)
    sin_up_ref[1] = jnp.zeros(ang.shape, f32)
    sin_dn_ref[1] = jnp.zeros(ang.shape, f32)


def _rope_tables(positions, rot_dim, *, tm=1024):
    T = positions.size
    half = rot_dim // 2
    inv_freq = 1.0 / (ROPE_THETA ** (jnp.arange(0, rot_dim, 2, dtype=f32) / rot_dim))
    invf = jnp.zeros((1, LANES), f32).at[0, :rot_dim].set(jnp.tile(inv_freq, 2))
    pos = positions.astype(f32).reshape(T, 1)
    tab = pl.BlockSpec((2, tm, LANES), lambda i: (0, i, 0))
    return pl.pallas_call(
        functools.partial(_rope_table_body, rot_half=half),
        out_shape=[jax.ShapeDtypeStruct((2, T, LANES), f32)] * 3,
        grid=(T // tm,),
        in_specs=[pl.BlockSpec((tm, 1), lambda i: (i, 0)),
                  pl.BlockSpec((1, LANES), lambda i: (0, 0))],
        out_specs=[tab, tab, tab],
        compiler_params=_params(1, 12 * _nbytes((tm, LANES), f32)),
        name="rope_tables",
    )(pos, invf)


def _diff_attn_body(q_ref, k_ref, v_ref, lam_ref, g_ref, o_ref, acc_ref, m_ref, l_ref,
                    *, tq, tk, scale, lambda_init):
    qi = pl.program_id(2)
    dh = DA_HEAD_DIM
    c2 = scale * math.log2(math.e)
    k_reps = tk // LANES
    v_reps = 2 * dh // LANES
    n_diag = tq // tk
    m_ref[...] = jnp.full(m_ref.shape, NEG_BIG, f32)
    l_ref[...] = jnp.zeros(l_ref.shape, f32)
    acc_ref[...] = jnp.zeros(acc_ref.shape, f32)

    def step(ki, diag):
        rows = pl.ds(pl.multiple_of(ki * tk, tk), tk)
        v = v_ref[rows, :]
        for c in range(2):
            s = lax.dot_general(q_ref[:, c * dh:(c + 1) * dh], k_ref[rows, c * dh:(c + 1) * dh],
                                (((1,), (1,)), ((), ())), preferred_element_type=f32)
            if diag is not None:
                r = lax.broadcasted_iota(jnp.int32, s.shape, 0)
                col = lax.broadcasted_iota(jnp.int32, s.shape, 1)
                s = jnp.where(col + diag * tk <= r, s, NEG_BIG)
            m_old = m_ref[c]
            m_new = jnp.maximum(m_old, jnp.max(s, axis=1, keepdims=True))
            a = jnp.exp2((m_old - m_new) * c2)
            p = jnp.exp2((s - jnp.tile(m_new, (1, k_reps))) * c2)
            l_ref[c] = a * l_ref[c] + jnp.sum(p, axis=1, keepdims=True)
            acc_ref[c] = (jnp.tile(a, (1, v_reps)) * acc_ref[c]
                          + jnp.dot(p.astype(bf16), v, preferred_element_type=f32))
            m_ref[c] = m_new

    n_below = qi * n_diag

    def below_pair(pi, carry):
        step(2 * pi, None)
        step(2 * pi + 1, None)
        return carry

    lax.fori_loop(0, n_below // 2, below_pair, 0)

    @pl.when(n_below % 2 == 1)
    def _():
        step(n_below - 1, None)
        step(n_below, 0)

    @pl.when(n_below % 2 == 0)
    def _():
        step(n_below, 0)

    for d in range(1, n_diag):
        step(n_below + d, d)

    lv = lam_ref[...]
    lam = (jnp.exp(jnp.sum(lv[0:1] * lv[1:2], axis=1, keepdims=True))
           - jnp.exp(jnp.sum(lv[2:3] * lv[3:4], axis=1, keepdims=True)) + lambda_init)
    o = (acc_ref[0] / jnp.tile(l_ref[0], (1, v_reps))
         - lam * (acc_ref[1] / jnp.tile(l_ref[1], (1, v_reps))))
    y = o * lax.rsqrt(jnp.mean(o * o, axis=1, keepdims=True) + NORM_EPS)
    o_ref[...] = (y * g_ref[...] * (1.0 - lambda_init)).astype(o_ref.dtype)


def _diff_attention(qkv, lam_vecs, subln_g, batch, seq, heads, lambda_init, *, tq=512, tk=512):
    T = qkv.shape[0]
    hw = 2 * DA_HEAD_DIM
    D = heads * hw
    nq = seq // tq
    vmem = (4 * _nbytes((seq, hw), bf16) + 4 * _nbytes((tq, hw), bf16)
            + 2 * _nbytes((tq, hw), f32) + 4 * _nbytes((tq, LANES), f32)
            + 8 * _nbytes((tq, tk), f32))
    return pl.pallas_call(
        functools.partial(_diff_attn_body, tq=tq, tk=tk, scale=DA_HEAD_DIM ** -0.5,
                          lambda_init=lambda_init),
        out_shape=jax.ShapeDtypeStruct((T, D), bf16),
        grid=(batch, heads, nq),
        in_specs=[pl.BlockSpec((tq, hw), lambda b, h, qi: (b * nq + qi, h)),
                  pl.BlockSpec((seq, hw), lambda b, h, qi: (b, heads + h)),
                  pl.BlockSpec((seq, hw), lambda b, h, qi: (b, 2 * heads + h)),
                  pl.BlockSpec((4, DA_HEAD_DIM), lambda b, h, qi: (0, 0)),
                  pl.BlockSpec((1, hw), lambda b, h, qi: (0, 0))],
        out_specs=pl.BlockSpec((tq, hw), lambda b, h, qi: (b * nq + qi, h)),
        scratch_shapes=[pltpu.VMEM((2, tq, hw), f32), pltpu.VMEM((2, tq, LANES), f32),
                        pltpu.VMEM((2, tq, LANES), f32)],
        compiler_params=_params(3, vmem),
        name="diff_attention",
    )(qkv, qkv, qkv, lam_vecs, subln_g.reshape(1, hw))


HALO = 32


def _ln_swish_body(v_ref, g_ref, b_ref, o_ref):
    y = _layer_norm_rows(v_ref[...], g_ref[...], b_ref[...])
    o_ref[...] = (y * jax.nn.sigmoid(y)).astype(o_ref.dtype)


def _ln_swish(v, ln_g, ln_b, *, tm=256):
    T, D = v.shape
    par = pl.BlockSpec((1, D), lambda i: (0, 0))
    return pl.pallas_call(
        _ln_swish_body,
        out_shape=jax.ShapeDtypeStruct((T, D), bf16),
        grid=(T // tm,),
        in_specs=[pl.BlockSpec((tm, D), lambda i: (i, 0)), par, par],
        out_specs=pl.BlockSpec((tm, D), lambda i: (i, 0)),
        compiler_params=_params(1, 6 * _nbytes((tm, D), f32)),
        name="ln_swish",
    )(v, ln_g.reshape(1, D), ln_b.reshape(1, D))


DMA_ISSUE_UNROLL = 8


def _gather_rows_body(idx_ref, src_ref, o_ref, buf_ref, sem, *, tg, n_steps):
    i = pl.program_id(0)

    def row_copy(step, r, slot):
        return pltpu.make_async_copy(
            src_ref.at[pl.ds(idx_ref[step * tg + r], 1), :],
            buf_ref.at[slot, pl.ds(r, 1), :], sem.at[slot])

    def start_tile(step):
        def start(g, carry):
            for u in range(DMA_ISSUE_UNROLL):
                row_copy(step, g * DMA_ISSUE_UNROLL + u, step % 2).start(
                    priority=u % N_DMA_PRIORITIES)
            return carry
        lax.fori_loop(0, tg // DMA_ISSUE_UNROLL, start, 0)

    @pl.when(i == 0)
    def _():
        start_tile(i)

    @pl.when(i + 1 < n_steps)
    def _():
        start_tile(i + 1)

    def wait(r, carry):
        row_copy(i, r, i % 2).wait()
        return carry

    lax.fori_loop(0, tg, wait, 0, unroll=DMA_ISSUE_UNROLL)
    o_ref[...] = buf_ref[i % 2].astype(o_ref.dtype)


def _gather_rows(src, idx, out_dtype, *, tg=256):
    P = idx.shape[0]
    D = src.shape[1]
    n_steps = P // tg
    return pl.pallas_call(
        functools.partial(_gather_rows_body, tg=tg, n_steps=n_steps),
        out_shape=jax.ShapeDtypeStruct((P, D), out_dtype),
        grid_spec=pltpu.PrefetchScalarGridSpec(
            num_scalar_prefetch=1, grid=(n_steps,),
            in_specs=[pl.BlockSpec(memory_space=pl.ANY)],
            out_specs=pl.BlockSpec((tg, D), lambda i, idx: (i, 0)),
            scratch_shapes=[pltpu.VMEM((2, tg, D), f32), pltpu.SemaphoreType.DMA((2,))]),
        compiler_params=_params(1, 6 * _nbytes((tg, D), f32)),
        name="gather_rows",
    )(idx, src)


def _moe_combine_body(pos_ref, x_ref, ye_ref, wts_ref, gate_ref, g_ref, b_ref, xo_ref,
                      buf_ref, sem, *, tm, n_steps, alpha):
    i = pl.program_id(0)

    def row_copy(step, r, k, slot):
        return pltpu.make_async_copy(
            ye_ref.at[pl.ds(pos_ref[(step * tm + r) * TOP_K + k], 1), :],
            buf_ref.at[slot, k, pl.ds(r, 1), :], sem.at[slot])

    def start_tile(step):
        def start(g, carry):
            for u in range(DMA_ISSUE_UNROLL):
                for k in range(TOP_K):
                    row_copy(step, g * DMA_ISSUE_UNROLL + u, k, step % 2).start(
                        priority=k % N_DMA_PRIORITIES)
            return carry
        lax.fori_loop(0, tm // DMA_ISSUE_UNROLL, start, 0)

    @pl.when(i == 0)
    def _():
        start_tile(i)

    @pl.when(i + 1 < n_steps)
    def _():
        start_tile(i + 1)

    def wait(r, carry):
        for k in range(TOP_K):
            row_copy(i, r, k, i % 2).wait()
        return carry

    lax.fori_loop(0, tm, wait, 0, unroll=DMA_ISSUE_UNROLL)
    wts = wts_ref[...]
    y = wts[:, 0:1] * buf_ref[i % 2, 0] + wts[:, 1:2] * buf_ref[i % 2, 1]
    z = alpha * x_ref[...] + (1.0 + gate_ref[...]) * y
    xo_ref[...] = _layer_norm_rows(z, g_ref[...], b_ref[...])


def _moe_combine_deepnorm(x2, ye, pos, wts, gate, g, b, seq, alpha, *, tm=256):
    T, D = x2.shape
    per_b = seq // tm
    return pl.pallas_call(
        functools.partial(_moe_combine_body, tm=tm, n_steps=T // tm, alpha=alpha),
        out_shape=jax.ShapeDtypeStruct((T, D), f32),
        grid_spec=pltpu.PrefetchScalarGridSpec(
            num_scalar_prefetch=1, grid=(T // tm,),
            in_specs=[pl.BlockSpec((tm, D), lambda i, p: (i, 0)),
                      pl.BlockSpec(memory_space=pl.ANY),
                      pl.BlockSpec((tm, LANES), lambda i, p: (i, 0)),
                      pl.BlockSpec((None, 1, D), lambda i, p: (i // per_b, 0, 0)),
                      pl.BlockSpec((1, D), lambda i, p: (0, 0)),
                      pl.BlockSpec((1, D), lambda i, p: (0, 0))],
            out_specs=pl.BlockSpec((tm, D), lambda i, p: (i, 0)),
            scratch_shapes=[pltpu.VMEM((2, TOP_K, tm, D), f32),
                            pltpu.SemaphoreType.DMA((2,))]),
        compiler_params=_params(1, 11 * _nbytes((tm, D), f32)),
        name="moe_combine_deepnorm",
    )(pos, x2, ye, wts, gate, g.reshape(1, D), b.reshape(1, D))


def _route_tables(top_i, tm):
    n_assign = top_i.size
    n_tok = top_i.shape[0]
    P = n_assign + N_EXPERTS * tm
    n_tiles = P // tm
    flat_e = top_i.reshape(-1)
    onehot = (flat_e[:, None] == jnp.arange(N_EXPERTS, dtype=jnp.int32)[None, :]).astype(jnp.int32)
    csum = jnp.cumsum(onehot, axis=0)
    rank = jnp.sum(onehot * (csum - 1), axis=1)
    counts = csum[-1]
    tiles_e = (counts + tm - 1) // tm
    tile_end = jnp.cumsum(tiles_e)
    starts = (tile_end - tiles_e) * tm
    pos = (starts[flat_e] + rank).astype(jnp.int32)
    token = jnp.arange(n_assign, dtype=jnp.int32) // (n_assign // n_tok)
    row_token = jnp.zeros((P,), jnp.int32).at[pos].set(token)
    n_used = tile_end[-1].astype(jnp.int32)
    tile_ids = jnp.minimum(jnp.arange(n_tiles, dtype=jnp.int32), n_used - 1)
    tile_expert = jnp.minimum(
        jnp.sum((tile_ids[:, None] >= tile_end[None, :]).astype(jnp.int32), axis=1),
        N_EXPERTS - 1).astype(jnp.int32)
    experts = jnp.arange(N_EXPERTS, dtype=jnp.int32)
    later_nonempty = (experts[None, :] > experts[:, None]) & (tiles_e[None, :] > 0)
    next_e = jnp.min(jnp.where(later_nonempty, experts[None, :], N_EXPERTS), axis=1)
    next_e = jnp.where(next_e < N_EXPERTS, next_e, -1).astype(jnp.int32)
    return pos, row_token, (tile_expert, n_used.reshape(1), next_e[tile_expert])


def kernel(x, c, positions, ada_w, ada_b, ln_g, ln_b, attn_w_qkv, attn_lambda, attn_subln_g, attn_w_o, conv_w_in, conv_b_in, conv_w_dw, conv_b_dw, conv_ln_g, conv_ln_b, conv_w_out, conv_b_out, ffn_w_in, ffn_w_out, moe_w_router, moe_w_in, moe_w_out):
    B, S, D = x.shape
    T = B * S
    depth = ada_w.shape[0]
    alpha = (2 * depth) ** 0.25
    heads = D // (2 * DA_HEAD_DIM)
    rot_dim = DA_HEAD_DIM // 4
    d_ff = ffn_w_out.shape[1]
    d_ffe = moe_w_out.shape[2]

    mods = _adaln(c, ada_w, ada_b)

    def mod(i, s):
        m = mods[2 * i + s]
        return [m[:, k * D:(k + 1) * D].reshape(B, 1, D) for k in range(3)]

    cos_t, sin_up, sin_dn = _rope_tables(positions, rot_dim)
    x2 = x.reshape(T, D)
    shift, scale, gate = mod(0, 0)
    h = _modulate(x2, scale, shift, S)

    for i in range(depth):
        jm = i // N_MIXERS
        if i % N_MIXERS == 0:
            lambda_init = 0.8 - 0.6 * math.exp(-0.3 * i)
            tn = 1024
            n_rope_blocks = 2 * D // tn
            tab = lambda j, kk, r: (jnp.where(j < n_rope_blocks, 0, 1), r, 0)
            qkv = _linear(
                h, [(attn_w_qkv[jm], 0)], functools.partial(_epi_rope, rot_half=rot_dim // 2),
                3 * D, bf16, tm=1024, tn=tn, row_split=4, name="qkv_rope",
                extras=[(t, (None, 1024, LANES), tab) for t in (cos_t, sin_up, sin_dn)])
            o = _diff_attention(qkv, attn_lambda[jm], attn_subln_g[jm], B, S, heads, lambda_init)
            y = _linear(o, [(attn_w_o[jm], 0)], _epi_plain, D, f32, tm=1024, tn=512,
                        name="attn_out")
        else:
            tm_c, tn_c, split = 1024, 512, 8
            nb = D // tn_c
            chan = lambda j, kk, r: (0, j)
            cv = _linear(h, [(conv_w_in[jm], 0), (conv_w_in[jm], nb)],
                         functools.partial(_epi_glu_dwconv, tiles_per_seq=S // tm_c, rb=64),
                         D, f32, tm=tm_c, tn=tn_c, row_split=split, name="conv_in_glu_dwconv",
                         biases=[(conv_b_in[jm].reshape(1, 2 * D), 0),
                                 (conv_b_in[jm].reshape(1, 2 * D), nb)],
                         extras=[(conv_w_dw[jm], (CONV_WIDTH, tn_c), chan),
                                 (conv_b_dw[jm].reshape(1, D), (1, tn_c), chan)],
                         epi_scratch=[(SUBLANES, tm_c // split + HALO, tn_c), (HALO, tn_c)])
            a = _ln_swish(cv, conv_ln_g[jm], conv_ln_b[jm])
            y = _linear(a, [(conv_w_out[jm], 0)], _epi_plain, D, f32, tm=1024, tn=512,
                        name="conv_out", biases=[(conv_b_out[jm].reshape(1, D), 0)])

        shift, scale, _ = mod(i, 1)
        jf = i // FFN_PERIOD
        is_moe = i % FFN_PERIOD != 0
        if not is_moe:
            x2, h = _deepnorm(x2, y, gate, ln_g[i, 0], ln_b[i, 0], S, alpha, nxt=(scale, shift))
        else:
            w_router = jnp.zeros((D, LANES), f32).at[:, :N_EXPERTS].set(moe_w_router[jf])
            x2, h32, idx, wts = _deepnorm(x2, y, gate, ln_g[i, 0], ln_b[i, 0], S, alpha,
                                          nxt=(scale, shift), router=w_router)
        _, _, gate = mod(i, 1)

        if not is_moe:
            nb = d_ff // 512
            act = _linear(h, [(ffn_w_in[jf], 0), (ffn_w_in[jf], nb)], _epi_swiglu, d_ff, bf16,
                          tm=1024, tn=512, row_split=2, name="ffn_in_swiglu")
            y = _linear(act, [(ffn_w_out[jf], 0)], _epi_plain, D, f32, tm=1024, tn=512,
                        n_k=4, name="ffn_out")
        else:
            tm_e = 256
            pos, row_token, groups = _route_tables(idx[:, :TOP_K], tm_e)
            hs = _gather_rows(h32, row_token, bf16)
            nb = d_ffe // 512
            act = _linear(hs, [(moe_w_in[jf], 0), (moe_w_in[jf], nb)], _epi_swiglu, d_ffe, bf16,
                          tm=tm_e, tn=512, name="moe_in_swiglu", group=groups)
            ye = _linear(act, [(moe_w_out[jf], 0)], _epi_plain, D, f32, tm=tm_e, tn=1024,
                         name="moe_out", group=groups)

        last = i == depth - 1
        if is_moe:
            x2 = _moe_combine_deepnorm(x2, ye, pos, wts, gate, ln_g[i, 1], ln_b[i, 1], S, alpha)
            if not last:
                shift, scale, gate = mod(i + 1, 0)
                h = _modulate(x2, scale, shift, S)
        elif last:
            (x2,) = _deepnorm(x2, y, gate, ln_g[i, 1], ln_b[i, 1], S, alpha)
        else:
            shift, scale, gate_next = mod(i + 1, 0)
            x2, h = _deepnorm(x2, y, gate, ln_g[i, 1], ln_b[i, 1], S, alpha, nxt=(scale, shift))
            gate = gate_next
    return x2.reshape(B, S, D)
```

```python
import functools
import math

import jax
import jax.numpy as jnp
from jax import lax
from jax.experimental import pallas as pl
from jax.experimental.pallas import tpu as pltpu

DA_HEAD_DIM = 128
ROPE_THETA = 500000.0
CONV_WIDTH = 31
N_EXPERTS = 8
TOP_K = 2
NORM_EPS = 1e-5
N_MIXERS = 2
FFN_PERIOD = 2

LANES = 128
SUBLANES = 8
V7X_VMEM_BYTES = 64 * 1024 * 1024
VMEM_REQUEST_CAP = V7X_VMEM_BYTES - 3 * 1024 * 1024
NEG_BIG = -1e30
N_DMA_PRIORITIES = 2
WEIGHT_DMA_PRIORITY = 1

f32 = jnp.float32
bf16 = jnp.bfloat16


def _params(n_grid, vmem_bytes):
    limit = min(int(vmem_bytes * 1.2) + (2 << 20), VMEM_REQUEST_CAP)
    return pltpu.CompilerParams(
        dimension_semantics=("arbitrary",) * n_grid, vmem_limit_bytes=limit)


def _nbytes(shape, dtype):
    return math.prod(shape) * jnp.dtype(dtype).itemsize


def _pack_bf16_pairs(h):
    n = h.shape[1] // 2
    bits = lax.bitcast_convert_type(h.astype(bf16).astype(f32), jnp.uint32)
    return (bits[:, n:] & jnp.uint32(0xFFFF0000)) | (bits[:, :n] >> 16)


def _unpack_bf16_pairs(w):
    lo = lax.bitcast_convert_type(w << 16, f32).astype(bf16)
    hi = lax.bitcast_convert_type(w & jnp.uint32(0xFFFF0000), f32).astype(bf16)
    return lo, hi


def _mm_body(*refs, n_pref, n_w, n_b, n_x, n_k, n_j, tk, tn, w_offs, row_split, packed_x,
             epilogue):
    pref = refs[:n_pref]
    x_ref = refs[n_pref]
    w_refs = refs[n_pref + 1:n_pref + 1 + n_w]
    b_refs = refs[n_pref + 1 + n_w:n_pref + 1 + n_w + n_b]
    e_refs = refs[n_pref + 1 + n_w + n_b:n_pref + 1 + n_w + n_b + n_x]
    o_ref = refs[n_pref + 1 + n_w + n_b + n_x]
    scratch = refs[n_pref + 2 + n_w + n_b + n_x:]
    stage_ref, wbf_ref, sem = scratch[:3]
    acc_ref = scratch[3] if n_k > 1 else None
    e_refs = tuple(e_refs) + tuple(scratch[4 if n_k > 1 else 3:])
    j, kk, i = pl.program_id(0), pl.program_id(1), pl.program_id(2)
    tm = x_ref.shape[0]

    if n_pref:
        te_ref, nu_ref, nx_ref = pref
        e_cur = te_ref[i]
        fresh = (i == 0) | (e_cur != te_ref[jnp.maximum(i - 1, 0)])
        valid = i < nu_ref[0]
        within = nx_ref[i] >= 0
        e_next = jnp.where(within, nx_ref[i], te_ref[0])
    else:
        e_cur = e_next = None
        fresh = i == 0
        valid = None
        within = jnp.bool_(False)
    more_k = kk + 1 < n_k
    stay = jnp.logical_or(within, more_k)
    kk_next = jnp.where(within, kk, jnp.where(more_k, kk + 1, 0))
    j_next = jnp.where(stay, j, j + 1)
    has_next = jnp.logical_or(stay, j + 1 < n_j)

    def w_copy(n, e, kk_, j_):
        src = w_refs[n] if e is None else w_refs[n].at[e]
        src = src.at[pl.ds(pl.multiple_of(kk_ * tk, tk), tk),
                     pl.ds(pl.multiple_of((w_offs[n] + j_) * tn, tn), tn)]
        return pltpu.make_async_copy(src, stage_ref.at[n], sem.at[n])

    @pl.when(fresh)
    def _():
        @pl.when((j == 0) & (kk == 0) & (i == 0))
        def _():
            for n in range(n_w):
                w_copy(n, e_cur, kk, j).start(priority=WEIGHT_DMA_PRIORITY)

        for n in range(n_w):
            w_copy(n, e_cur, kk, j).wait()
            wbf_ref[n] = stage_ref[n].astype(bf16)

        @pl.when(has_next)
        def _():
            for n in range(n_w):
                w_copy(n, e_next, kk_next, j_next).start(priority=WEIGHT_DMA_PRIORITY)

    def compute():
        if n_k == 1:
            sub = tm // row_split
            for s in range(row_split):
                rs = slice(s * sub, (s + 1) * sub)
                if packed_x:
                    lo, hi = _unpack_bf16_pairs(x_ref[rs, :])
                    half = tk // 2
                    accs = [jnp.dot(lo, wbf_ref[n, :half, :], preferred_element_type=f32)
                            + jnp.dot(hi, wbf_ref[n, half:, :], preferred_element_type=f32)
                            for n in range(n_w)]
                else:
                    x = x_ref[rs, :]
                    accs = [jnp.dot(x, wbf_ref[n], preferred_element_type=f32)
                            for n in range(n_w)]
                epilogue(accs, b_refs, e_refs, o_ref, rs)
            return
        rows = pl.ds(pl.multiple_of(i * tm, tm), tm)

        def products():
            x = x_ref[...]
            return [jnp.dot(x, wbf_ref[n], preferred_element_type=f32) for n in range(n_w)]

        @pl.when(kk == 0)
        def _():
            for n, a in enumerate(products()):
                acc_ref[n, rows, :] = a

        @pl.when((kk > 0) & (kk < n_k - 1))
        def _():
            for n, a in enumerate(products()):
                acc_ref[n, rows, :] += a

        @pl.when(kk == n_k - 1)
        def _():
            epilogue([acc_ref[n, rows, :] + a for n, a in enumerate(products())],
                     b_refs, e_refs, o_ref, slice(0, tm))

    if valid is None:
        compute()
    else:
        pl.when(valid)(compute)

        @pl.when(jnp.logical_not(valid))
        def _():
            o_ref[...] = jnp.zeros(o_ref.shape, o_ref.dtype)


def _linear(x, weights, epilogue, out_cols, out_dtype, *, tm, tn, n_k=1, row_split=1,
            biases=(), extras=(), epi_scratch=(), group=None, name):
    packed_x = x.dtype == jnp.uint32
    M = x.shape[0]
    K = x.shape[1] * (2 if packed_x else 1)
    tk = K // n_k
    x_cols = x.shape[1] // n_k
    assert M % tm == 0 and K % n_k == 0 and out_cols % tn == 0 and tm % row_split == 0
    assert n_k == 1 or not packed_x
    n_w, n_b = len(weights), len(biases)
    n_pref = len(group) if group is not None else 0
    n_j = out_cols // tn
    grid = (n_j, n_k, M // tm)
    last_k = n_k - 1

    if group is None:
        def row(i, *pref):
            return i
    else:
        def row(i, te, nu, nx):
            return jnp.minimum(i, nu[0] - 1)

    in_specs = [pl.BlockSpec((tm, x_cols), lambda j, kk, i, *p: (row(i, *p), kk))]
    args = [x]
    for w, _ in weights:
        in_specs.append(pl.BlockSpec(memory_space=pl.ANY))
        args.append(w)
    for b, off in biases:
        in_specs.append(pl.BlockSpec((1, tn), lambda j, kk, i, *p, off=off: (0, off + j)))
        args.append(b)
    for arr, blk, imap in extras:
        in_specs.append(pl.BlockSpec(blk, imap))
        args.append(arr)

    if n_k == 1:
        out_map = lambda j, kk, i, *p: (i, j)
    else:
        assert group is None
        out_map = lambda j, kk, i: (jnp.where(kk == last_k, i, 0), j)
    out_spec = pl.BlockSpec((tm, tn), out_map)

    scratch = [pltpu.VMEM((n_w, tk, tn), f32), pltpu.VMEM((n_w, tk, tn), bf16),
               pltpu.SemaphoreType.DMA((n_w,))]
    vmem = (2 * _nbytes((tm, tk), bf16) + n_w * _nbytes((tk, tn), f32)
            + n_w * _nbytes((tk, tn), bf16) + 2 * _nbytes((tm, tn), out_dtype)
            + 3 * n_w * _nbytes((tm // row_split, tn), f32))
    for arr, blk, _ in extras:
        vmem += 2 * _nbytes([d for d in blk if d is not None], arr.dtype)
    if n_k > 1:
        scratch.append(pltpu.VMEM((n_w, M, tn), f32))
        vmem += n_w * _nbytes((M, tn), f32)
    for shape in epi_scratch:
        scratch.append(pltpu.VMEM(shape, f32))
        vmem += _nbytes(shape, f32)

    body = functools.partial(_mm_body, n_pref=n_pref, n_w=n_w, n_b=n_b, n_x=len(extras),
                             n_k=n_k, n_j=n_j, tk=tk, tn=tn,
                             w_offs=[off for _, off in weights], row_split=row_split,
                             packed_x=packed_x, epilogue=epilogue)
    call = pl.pallas_call(
        body,
        out_shape=jax.ShapeDtypeStruct((M, out_cols), out_dtype),
        grid_spec=pltpu.PrefetchScalarGridSpec(
            num_scalar_prefetch=n_pref, grid=grid, in_specs=in_specs, out_specs=out_spec,
            scratch_shapes=scratch),
        compiler_params=_params(3, vmem),
        name=name,
    )
    pref_args = list(group) if group is not None else []
    return call(*pref_args, *args)


def _epi_plain(accs, b_refs, e_refs, o_ref, rs):
    y = accs[0]
    if b_refs:
        y = y + b_refs[0][...]
    o_ref[rs, :] = y.astype(o_ref.dtype)


def _epi_swiglu(accs, b_refs, e_refs, o_ref, rs):
    g, u = accs
    o_ref[rs, :] = (g * jax.nn.sigmoid(g) * u).astype(o_ref.dtype)


def _epi_glu_dwconv(accs, b_refs, e_refs, o_ref, rs, *, tiles_per_seq, rb):
    w_ref, bdw_ref, win_ref, carry_ref = e_refs
    i = pl.program_id(2)
    sub = rs.stop - rs.start
    a = accs[0] + b_refs[0][...]
    g = accs[1] + b_refs[1][...]
    u = a * jax.nn.sigmoid(g)
    halo = carry_ref[...]
    if rs.start == 0:
        halo = jnp.where(i % tiles_per_seq == 0, jnp.zeros_like(halo), halo)
    win_ref[0, 0:HALO, :] = halo
    win_ref[0, HALO:, :] = u
    carry_ref[...] = u[sub - HALO:, :]
    first = HALO - (CONV_WIDTH - 1)
    span = sub + HALO - SUBLANES
    for r in range(1, SUBLANES):
        win_ref[r, 0:span, :] = win_ref[0, r:r + span, :]
    for r0 in range(0, sub, rb):
        acc = jnp.zeros((rb, u.shape[1]), f32) + bdw_ref[...]
        for tap in range(CONV_WIDTH):
            r = (first + tap) % SUBLANES
            base = r0 + first + tap - r
            acc = acc + w_ref[tap:tap + 1, :] * win_ref[r, base:base + rb, :]
        o_ref[rs.start + r0:rs.start + r0 + rb, :] = acc


def _epi_rope(accs, b_refs, e_refs, o_ref, rs, *, rot_half):
    y = accs[0]
    c, su, sd = (e[rs, :] for e in e_refs)
    for g in range(y.shape[1] // LANES):
        yg = y[:, g * LANES:(g + 1) * LANES]
        up = pltpu.roll(yg, rot_half, axis=1)
        dn = pltpu.roll(yg, LANES - rot_half, axis=1)
        o_ref[rs, g * LANES:(g + 1) * LANES] = (yg * c + up * su + dn * sd).astype(o_ref.dtype)


def _adaln_body(c_ref, w_ref, b_ref, o_ref):
    c = c_ref[...]
    sc = (c * jax.nn.sigmoid(c)).astype(bf16)
    y = jnp.dot(sc, w_ref[...].astype(bf16), preferred_element_type=f32)
    o_ref[...] = y + b_ref[...]


def _adaln(c, ada_w, ada_b, *, tn=512):
    B, D = c.shape
    L = ada_w.shape[0] * ada_w.shape[1]
    w = ada_w.reshape(L, D, 3 * D)
    b = ada_b.reshape(L, 1, 3 * D)
    rows = 8
    c_pad = jnp.zeros((rows, D), f32).at[:B].set(c)
    vmem = 2 * _nbytes((D, tn), f32) + _nbytes((D, tn), bf16) + 4 * _nbytes((rows, D), f32)
    out = pl.pallas_call(
        _adaln_body,
        out_shape=jax.ShapeDtypeStruct((L, rows, 3 * D), f32),
        grid=(L, 3 * D // tn),
        in_specs=[pl.BlockSpec((rows, D), lambda m, j: (0, 0)),
                  pl.BlockSpec((None, D, tn), lambda m, j: (m, 0, j)),
                  pl.BlockSpec((None, 1, tn), lambda m, j: (m, 0, j))],
        out_specs=pl.BlockSpec((None, rows, tn), lambda m, j: (m, 0, j)),
        compiler_params=_params(2, vmem),
        name="adaln",
    )(c_pad, w, b)
    return out[:, :B]


def _modulate_body(x_ref, scale_ref, shift_ref, h_ref):
    h_ref[...] = (x_ref[...] * (1.0 + scale_ref[...]) + shift_ref[...]).astype(h_ref.dtype)


def _modulate(x2, scale, shift, seq, *, tm=512):
    T, D = x2.shape
    per_b = seq // tm
    vec = pl.BlockSpec((None, 1, D), lambda i: (i // per_b, 0, 0))
    return pl.pallas_call(
        _modulate_body,
        out_shape=jax.ShapeDtypeStruct((T, D), bf16),
        grid=(T // tm,),
        in_specs=[pl.BlockSpec((tm, D), lambda i: (i, 0)), vec, vec],
        out_specs=pl.BlockSpec((tm, D), lambda i: (i, 0)),
        compiler_params=_params(1, 2 * _nbytes((tm, D), f32) * 2),
        name="modulate",
    )(x2, scale, shift)


def _layer_norm_rows(z, g, b):
    mu = jnp.mean(z, axis=-1, keepdims=True)
    zc = z - mu
    var = jnp.mean(zc * zc, axis=-1, keepdims=True)
    return zc * lax.rsqrt(var + NORM_EPS) * g + b


def _top2_route(logits, n_experts):
    lane_i = lax.broadcasted_iota(jnp.int32, logits.shape, 1)
    lane = lane_i.astype(f32)
    lg = jnp.where(lane_i < n_experts, logits, -jnp.inf)
    m1 = jnp.max(lg, axis=1, keepdims=True)
    i1 = jnp.min(jnp.where(lg == m1, lane, float(LANES)), axis=1, keepdims=True)
    lg2 = jnp.where(lane == i1, -jnp.inf, lg)
    m2 = jnp.max(lg2, axis=1, keepdims=True)
    i2 = jnp.min(jnp.where(lg2 == m2, lane, float(LANES)), axis=1, keepdims=True)
    e = jnp.exp(m2 - m1)
    w1 = 1.0 / (1.0 + e)
    w2 = e / (1.0 + e)
    idx = jnp.where(lane_i == 0, i1, jnp.where(lane_i == 1, i2, 0.0)).astype(jnp.int32)
    wts = jnp.where(lane_i == 0, w1, jnp.where(lane_i == 1, w2, 0.0))
    return idx, wts


def _deepnorm_body(*refs, alpha, mode):
    x_ref, y_ref, gate_ref, g_ref, b_ref = refs[:5]
    z = alpha * x_ref[...] + (1.0 + gate_ref[...]) * y_ref[...]
    xo = _layer_norm_rows(z, g_ref[...], b_ref[...])
    if mode == "last":
        refs[5][...] = xo
        return
    scale_ref, shift_ref = refs[5:7]
    h = xo * (1.0 + scale_ref[...]) + shift_ref[...]
    if mode == "next":
        xo_ref, h_ref = refs[7:9]
        xo_ref[...] = xo
        h_ref[...] = h.astype(h_ref.dtype)
        return
    wr_ref, xo_ref, h_ref, idx_ref, wts_ref = refs[7:12]
    xo_ref[...] = xo
    h_ref[...] = _pack_bf16_pairs(h)
    logits = jnp.dot(h, wr_ref[...], preferred_element_type=f32,
                     precision=lax.Precision.HIGHEST)
    idx, wts = _top2_route(logits, N_EXPERTS)
    idx_ref[...] = idx
    wts_ref[...] = wts


def _deepnorm(x2, y2, gate, g, b, seq, alpha, *, nxt=None, router=None, tm=256):
    T, D = x2.shape
    per_b = seq // tm
    rowblk = pl.BlockSpec((tm, D), lambda i: (i, 0))
    vec = pl.BlockSpec((None, 1, D), lambda i: (i // per_b, 0, 0))
    par = pl.BlockSpec((1, D), lambda i: (0, 0))
    lane_blk = pl.BlockSpec((tm, LANES), lambda i: (i, 0))
    in_specs = [rowblk, rowblk, vec, par, par]
    args = [x2, y2, gate, g.reshape(1, D), b.reshape(1, D)]
    out_shape = [jax.ShapeDtypeStruct((T, D), f32)]
    out_specs = [rowblk]
    n_blk = 3
    mode = "last"
    if nxt is not None:
        mode = "next"
        in_specs += [vec, vec]
        args += list(nxt)
        if router is None:
            out_shape.append(jax.ShapeDtypeStruct((T, D), bf16))
            out_specs.append(rowblk)
        else:
            out_shape.append(jax.ShapeDtypeStruct((T, D // 2), jnp.uint32))
            out_specs.append(pl.BlockSpec((tm, D // 2), lambda i: (i, 0)))
        n_blk = 4
    if router is not None:
        mode = "router"
        in_specs.append(pl.BlockSpec((D, LANES), lambda i: (0, 0)))
        args.append(router)
        out_shape += [jax.ShapeDtypeStruct((T, LANES), jnp.int32),
                      jax.ShapeDtypeStruct((T, LANES), f32)]
        out_specs += [lane_blk, lane_blk]
    vmem = (2 * n_blk + 3) * _nbytes((tm, D), f32) + 2 * _nbytes((D, LANES), f32)
    return pl.pallas_call(
        functools.partial(_deepnorm_body, alpha=alpha, mode=mode),
        out_shape=out_shape,
        grid=(T // tm,),
        in_specs=in_specs,
        out_specs=out_specs,
        compiler_params=_params(1, vmem),
        name="deepnorm_" + mode,
    )(*args)


def _rope_table_body(pos_ref, invf_ref, cos_ref, sin_up_ref, sin_dn_ref, *, rot_half):
    ang = pos_ref[...] * invf_ref[...]
    lane = lax.broadcasted_iota(jnp.int32, ang.shape, 1)
    c, s = jnp.cos(ang), jnp.sin(ang)
    cos_ref[0] = jnp.where(lane < 2 * rot_half, c, 1.0)
    sin_up_ref[0] = jnp.where((lane >= rot_half) & (lane < 2 * rot_half), s, 0.0)
    sin_dn_ref[0] = jnp.where(lane < rot_half, -s, 0.0)
    cos_ref[1] = jnp.ones(ang.shape, f32)
    sin_up_ref[1] = jnp.zeros(ang.shape, f32)
    sin_dn_ref[1] = jnp.zeros(ang.shape, f32)


def _rope_tables(positions, rot_dim, *, tm=1024):
    T = positions.size
    half = rot_dim // 2
    inv_freq = 1.0 / (ROPE_THETA ** (jnp.arange(0, rot_dim, 2, dtype=f32) / rot_dim))
    invf = jnp.zeros((1, LANES), f32).at[0, :rot_dim].set(jnp.tile(inv_freq, 2))
    pos = positions.astype(f32).reshape(T, 1)
    tab = pl.BlockSpec((2, tm, LANES), lambda i: (0, i, 0))
    return pl.pallas_call(
        functools.partial(_rope_table_body, rot_half=half),
        out_shape=[jax.ShapeDtypeStruct((2, T, LANES), f32)] * 3,
        grid=(T // tm,),
        in_specs=[pl.BlockSpec((tm, 1), lambda i: (i, 0)),
                  pl.BlockSpec((1, LANES), lambda i: (0, 0))],
        out_specs=[tab, tab, tab],
        compiler_params=_params(1, 12 * _nbytes((tm, LANES), f32)),
        name="rope_tables",
    )(pos, invf)


def _diff_attn_body(q_ref, k_ref, v_ref, lam_ref, g_ref, o_ref, acc_ref, m_ref, l_ref,
                    *, tq, tk, scale, lambda_init):
    qi = pl.program_id(2)
    dh = DA_HEAD_DIM
    c2 = scale * math.log2(math.e)
    k_reps = tk // LANES
    v_reps = 2 * dh // LANES
    n_diag = tq // tk
    m_ref[...] = jnp.full(m_ref.shape, NEG_BIG, f32)
    l_ref[...] = jnp.zeros(l_ref.shape, f32)
    acc_ref[...] = jnp.zeros(acc_ref.shape, f32)

    def step(ki, diag):
        rows = pl.ds(pl.multiple_of(ki * tk, tk), tk)
        v = v_ref[rows, :]
        for c in range(2):
            s = lax.dot_general(q_ref[:, c * dh:(c + 1) * dh], k_ref[rows, c * dh:(c + 1) * dh],
                                (((1,), (1,)), ((), ())), preferred_element_type=f32)
            if diag is not None:
                r = lax.broadcasted_iota(jnp.int32, s.shape, 0)
                col = lax.broadcasted_iota(jnp.int32, s.shape, 1)
                s = jnp.where(col + diag * tk <= r, s, NEG_BIG)
            m_old = m_ref[c]
            m_new = jnp.maximum(m_old, jnp.max(s, axis=1, keepdims=True))
            a = jnp.exp2((m_old - m_new) * c2)
            p = jnp.exp2((s - jnp.tile(m_new, (1, k_reps))) * c2)
            l_ref[c] = a * l_ref[c] + jnp.sum(p, axis=1, keepdims=True)
            acc_ref[c] = (jnp.tile(a, (1, v_reps)) * acc_ref[c]
                          + jnp.dot(p.astype(bf16), v, preferred_element_type=f32))
            m_ref[c] = m_new

    n_below = qi * n_diag

    def below_pair(pi, carry):
        step(2 * pi, None)
        step(2 * pi + 1, None)
        return carry

    lax.fori_loop(0, n_below // 2, below_pair, 0)

    @pl.when(n_below % 2 == 1)
    def _():
        step(n_below - 1, None)
        step(n_below, 0)

    @pl.when(n_below % 2 == 0)
    def _():
        step(n_below, 0)

    for d in range(1, n_diag):
        step(n_below + d, d)

    lv = lam_ref[...]
    lam = (jnp.exp(jnp.sum(lv[0:1] * lv[1:2], axis=1, keepdims=True))
           - jnp.exp(jnp.sum(lv[2:3] * lv[3:4], axis=1, keepdims=True)) + lambda_init)
    o = (acc_ref[0] / jnp.tile(l_ref[0], (1, v_reps))
         - lam * (acc_ref[1] / jnp.tile(l_ref[1], (1, v_reps))))
    y = o * lax.rsqrt(jnp.mean(o * o, axis=1, keepdims=True) + NORM_EPS)
    o_ref[...] = (y * g_ref[...] * (1.0 - lambda_init)).astype(o_ref.dtype)


def _diff_attention(qkv, lam_vecs, subln_g, batch, seq, heads, lambda_init, *, tq=512, tk=512):
    T = qkv.shape[0]
    hw = 2 * DA_HEAD_DIM
    D = heads * hw
    nq = seq // tq
    vmem = (4 * _nbytes((seq, hw), bf16) + 4 * _nbytes((tq, hw), bf16)
            + 2 * _nbytes((tq, hw), f32) + 4 * _nbytes((tq, LANES), f32)
            + 8 * _nbytes((tq, tk), f32))
    return pl.pallas_call(
        functools.partial(_diff_attn_body, tq=tq, tk=tk, scale=DA_HEAD_DIM ** -0.5,
                          lambda_init=lambda_init),
        out_shape=jax.ShapeDtypeStruct((T, D), bf16),
        grid=(batch, heads, nq),
        in_specs=[pl.BlockSpec((tq, hw), lambda b, h, qi: (b * nq + qi, h)),
                  pl.BlockSpec((seq, hw), lambda b, h, qi: (b, heads + h)),
                  pl.BlockSpec((seq, hw), lambda b, h, qi: (b, 2 * heads + h)),
                  pl.BlockSpec((4, DA_HEAD_DIM), lambda b, h, qi: (0, 0)),
                  pl.BlockSpec((1, hw), lambda b, h, qi: (0, 0))],
        out_specs=pl.BlockSpec((tq, hw), lambda b, h, qi: (b * nq + qi, h)),
        scratch_shapes=[pltpu.VMEM((2, tq, hw), f32), pltpu.VMEM((2, tq, LANES), f32),
                        pltpu.VMEM((2, tq, LANES), f32)],
        compiler_params=_params(3, vmem),
        name="diff_attention",
    )(qkv, qkv, qkv, lam_vecs, subln_g.reshape(1, hw))


HALO = 32


def _ln_swish_body(v_ref, g_ref, b_ref, o_ref):
    y = _layer_norm_rows(v_ref[...], g_ref[...], b_ref[...])
    o_ref[...] = (y * jax.nn.sigmoid(y)).astype(o_ref.dtype)


def _ln_swish(v, ln_g, ln_b, *, tm=256):
    T, D = v.shape
    par = pl.BlockSpec((1, D), lambda i: (0, 0))
    return pl.pallas_call(
        _ln_swish_body,
        out_shape=jax.ShapeDtypeStruct((T, D), bf16),
        grid=(T // tm,),
        in_specs=[pl.BlockSpec((tm, D), lambda i: (i, 0)), par, par],
        out_specs=pl.BlockSpec((tm, D), lambda i: (i, 0)),
        compiler_params=_params(1, 6 * _nbytes((tm, D), f32)),
        name="ln_swish",
    )(v, ln_g.reshape(1, D), ln_b.reshape(1, D))


DMA_ISSUE_UNROLL = 8


def _gather_rows_body(idx_ref, src_ref, o_ref, buf_ref, sem, *, tg, n_steps):
    i = pl.program_id(0)

    def row_copy(step, r, slot):
        return pltpu.make_async_copy(
            src_ref.at[pl.ds(idx_ref[step * tg + r], 1), :],
            buf_ref.at[slot, pl.ds(r, 1), :], sem.at[slot])

    def start_tile(step):
        def start(g, carry):
            for u in range(DMA_ISSUE_UNROLL):
                row_copy(step, g * DMA_ISSUE_UNROLL + u, step % 2).start(
                    priority=u % N_DMA_PRIORITIES)
            return carry
        lax.fori_loop(0, tg // DMA_ISSUE_UNROLL, start, 0)

    @pl.when(i == 0)
    def _():
        start_tile(i)

    @pl.when(i + 1 < n_steps)
    def _():
        start_tile(i + 1)

    def wait(r, carry):
        row_copy(i, r, i % 2).wait()
        return carry

    lax.fori_loop(0, tg, wait, 0, unroll=DMA_ISSUE_UNROLL)
    o_ref[...] = buf_ref[i % 2]


def _gather_rows(src, idx, *, tg=256):
    assert jnp.dtype(src.dtype).itemsize == 4
    P = idx.shape[0]
    D = src.shape[1]
    n_steps = P // tg
    return pl.pallas_call(
        functools.partial(_gather_rows_body, tg=tg, n_steps=n_steps),
        out_shape=jax.ShapeDtypeStruct((P, D), src.dtype),
        grid_spec=pltpu.PrefetchScalarGridSpec(
            num_scalar_prefetch=1, grid=(n_steps,),
            in_specs=[pl.BlockSpec(memory_space=pl.ANY)],
            out_specs=pl.BlockSpec((tg, D), lambda i, idx: (i, 0)),
            scratch_shapes=[pltpu.VMEM((2, tg, D), src.dtype),
                            pltpu.SemaphoreType.DMA((2,))]),
        compiler_params=_params(1, 6 * _nbytes((tg, D), src.dtype)),
        name="gather_rows",
    )(idx, src)


def _moe_combine_body(pos_ref, x_ref, ye_ref, wts_ref, gate_ref, g_ref, b_ref, xo_ref,
                      buf_ref, sem, *, tm, n_steps, alpha):
    i = pl.program_id(0)

    def row_copy(step, r, k, slot):
        return pltpu.make_async_copy(
            ye_ref.at[pl.ds(pos_ref[(step * tm + r) * TOP_K + k], 1), :],
            buf_ref.at[slot, k, pl.ds(r, 1), :], sem.at[slot])

    def start_tile(step):
        def start(g, carry):
            for u in range(DMA_ISSUE_UNROLL):
                for k in range(TOP_K):
                    row_copy(step, g * DMA_ISSUE_UNROLL + u, k, step % 2).start(
                        priority=k % N_DMA_PRIORITIES)
            return carry
        lax.fori_loop(0, tm // DMA_ISSUE_UNROLL, start, 0)

    @pl.when(i == 0)
    def _():
        start_tile(i)

    @pl.when(i + 1 < n_steps)
    def _():
        start_tile(i + 1)

    def wait(r, carry):
        for k in range(TOP_K):
            row_copy(i, r, k, i % 2).wait()
        return carry

    lax.fori_loop(0, tm, wait, 0, unroll=DMA_ISSUE_UNROLL)
    wts = wts_ref[...]
    y = wts[:, 0:1] * buf_ref[i % 2, 0] + wts[:, 1:2] * buf_ref[i % 2, 1]
    z = alpha * x_ref[...] + (1.0 + gate_ref[...]) * y
    xo_ref[...] = _layer_norm_rows(z, g_ref[...], b_ref[...])


def _moe_combine_deepnorm(x2, ye, pos, wts, gate, g, b, seq, alpha, *, tm=256):
    T, D = x2.shape
    per_b = seq // tm
    return pl.pallas_call(
        functools.partial(_moe_combine_body, tm=tm, n_steps=T // tm, alpha=alpha),
        out_shape=jax.ShapeDtypeStruct((T, D), f32),
        grid_spec=pltpu.PrefetchScalarGridSpec(
            num_scalar_prefetch=1, grid=(T // tm,),
            in_specs=[pl.BlockSpec((tm, D), lambda i, p: (i, 0)),
                      pl.BlockSpec(memory_space=pl.ANY),
                      pl.BlockSpec((tm, LANES), lambda i, p: (i, 0)),
                      pl.BlockSpec((None, 1, D), lambda i, p: (i // per_b, 0, 0)),
                      pl.BlockSpec((1, D), lambda i, p: (0, 0)),
                      pl.BlockSpec((1, D), lambda i, p: (0, 0))],
            out_specs=pl.BlockSpec((tm, D), lambda i, p: (i, 0)),
            scratch_shapes=[pltpu.VMEM((2, TOP_K, tm, D), f32),
                            pltpu.SemaphoreType.DMA((2,))]),
        compiler_params=_params(1, 11 * _nbytes((tm, D), f32)),
        name="moe_combine_deepnorm",
    )(pos, x2, ye, wts, gate, g.reshape(1, D), b.reshape(1, D))


def _route_tables(top_i, tm):
    n_assign = top_i.size
    n_tok = top_i.shape[0]
    P = n_assign + N_EXPERTS * tm
    n_tiles = P // tm
    flat_e = top_i.reshape(-1)
    onehot = (flat_e[:, None] == jnp.arange(N_EXPERTS, dtype=jnp.int32)[None, :]).astype(jnp.int32)
    csum = jnp.cumsum(onehot, axis=0)
    rank = jnp.sum(onehot * (csum - 1), axis=1)
    counts = csum[-1]
    tiles_e = (counts + tm - 1) // tm
    tile_end = jnp.cumsum(tiles_e)
    starts = (tile_end - tiles_e) * tm
    pos = (starts[flat_e] + rank).astype(jnp.int32)
    token = jnp.arange(n_assign, dtype=jnp.int32) // (n_assign // n_tok)
    row_token = jnp.zeros((P,), jnp.int32).at[pos].set(token)
    n_used = tile_end[-1].astype(jnp.int32)
    tile_ids = jnp.minimum(jnp.arange(n_tiles, dtype=jnp.int32), n_used - 1)
    tile_expert = jnp.minimum(
        jnp.sum((tile_ids[:, None] >= tile_end[None, :]).astype(jnp.int32), axis=1),
        N_EXPERTS - 1).astype(jnp.int32)
    experts = jnp.arange(N_EXPERTS, dtype=jnp.int32)
    later_nonempty = (experts[None, :] > experts[:, None]) & (tiles_e[None, :] > 0)
    next_e = jnp.min(jnp.where(later_nonempty, experts[None, :], N_EXPERTS), axis=1)
    next_e = jnp.where(next_e < N_EXPERTS, next_e, -1).astype(jnp.int32)
    return pos, row_token, (tile_expert, n_used.reshape(1), next_e[tile_expert])


def kernel(x, c, positions, ada_w, ada_b, ln_g, ln_b, attn_w_qkv, attn_lambda, attn_subln_g, attn_w_o, conv_w_in, conv_b_in, conv_w_dw, conv_b_dw, conv_ln_g, conv_ln_b, conv_w_out, conv_b_out, ffn_w_in, ffn_w_out, moe_w_router, moe_w_in, moe_w_out):
    B, S, D = x.shape
    T = B * S
    depth = ada_w.shape[0]
    alpha = (2 * depth) ** 0.25
    heads = D // (2 * DA_HEAD_DIM)
    rot_dim = DA_HEAD_DIM // 4
    d_ff = ffn_w_out.shape[1]
    d_ffe = moe_w_out.shape[2]

    mods = _adaln(c, ada_w, ada_b)

    def mod(i, s):
        m = mods[2 * i + s]
        return [m[:, k * D:(k + 1) * D].reshape(B, 1, D) for k in range(3)]

    cos_t, sin_up, sin_dn = _rope_tables(positions, rot_dim)
    x2 = x.reshape(T, D)
    shift, scale, gate = mod(0, 0)
    h = _modulate(x2, scale, shift, S)

    for i in range(depth):
        jm = i // N_MIXERS
        if i % N_MIXERS == 0:
            lambda_init = 0.8 - 0.6 * math.exp(-0.3 * i)
            tn = 1024
            n_rope_blocks = 2 * D // tn
            tab = lambda j, kk, r: (jnp.where(j < n_rope_blocks, 0, 1), r, 0)
            qkv = _linear(
                h, [(attn_w_qkv[jm], 0)], functools.partial(_epi_rope, rot_half=rot_dim // 2),
                3 * D, bf16, tm=1024, tn=tn, row_split=4, name="qkv_rope",
                extras=[(t, (None, 1024, LANES), tab) for t in (cos_t, sin_up, sin_dn)])
            o = _diff_attention(qkv, attn_lambda[jm], attn_subln_g[jm], B, S, heads, lambda_init)
            y = _linear(o, [(attn_w_o[jm], 0)], _epi_plain, D, f32, tm=1024, tn=512,
                        name="attn_out")
        else:
            tm_c, tn_c, split = 1024, 512, 8
            nb = D // tn_c
            chan = lambda j, kk, r: (0, j)
            cv = _linear(h, [(conv_w_in[jm], 0), (conv_w_in[jm], nb)],
                         functools.partial(_epi_glu_dwconv, tiles_per_seq=S // tm_c, rb=64),
                         D, f32, tm=tm_c, tn=tn_c, row_split=split, name="conv_in_glu_dwconv",
                         biases=[(conv_b_in[jm].reshape(1, 2 * D), 0),
                                 (conv_b_in[jm].reshape(1, 2 * D), nb)],
                         extras=[(conv_w_dw[jm], (CONV_WIDTH, tn_c), chan),
                                 (conv_b_dw[jm].reshape(1, D), (1, tn_c), chan)],
                         epi_scratch=[(SUBLANES, tm_c // split + HALO, tn_c), (HALO, tn_c)])
            a = _ln_swish(cv, conv_ln_g[jm], conv_ln_b[jm])
            y = _linear(a, [(conv_w_out[jm], 0)], _epi_plain, D, f32, tm=1024, tn=512,
                        name="conv_out", biases=[(conv_b_out[jm].reshape(1, D), 0)])

        shift, scale, _ = mod(i, 1)
        jf = i // FFN_PERIOD
        is_moe = i % FFN_PERIOD != 0
        if not is_moe:
            x2, h = _deepnorm(x2, y, gate, ln_g[i, 0], ln_b[i, 0], S, alpha, nxt=(scale, shift))
        else:
            w_router = jnp.zeros((D, LANES), f32).at[:, :N_EXPERTS].set(moe_w_router[jf])
            x2, h_pairs, idx, wts = _deepnorm(x2, y, gate, ln_g[i, 0], ln_b[i, 0], S, alpha,
                                          nxt=(scale, shift), router=w_router)
        _, _, gate = mod(i, 1)

        if not is_moe:
            nb = d_ff // 512
            act = _linear(h, [(ffn_w_in[jf], 0), (ffn_w_in[jf], nb)], _epi_swiglu, d_ff, bf16,
                          tm=1024, tn=512, row_split=2, name="ffn_in_swiglu")
            y = _linear(act, [(ffn_w_out[jf], 0)], _epi_plain, D, f32, tm=1024, tn=512,
                        n_k=4, name="ffn_out")
        else:
            tm_e = 256
            pos, row_token, groups = _route_tables(idx[:, :TOP_K], tm_e)
            hs = _gather_rows(h_pairs, row_token)
            nb = d_ffe // 512
            act = _linear(hs, [(moe_w_in[jf], 0), (moe_w_in[jf], nb)], _epi_swiglu, d_ffe, bf16,
                          tm=tm_e, tn=512, name="moe_in_swiglu", group=groups)
            ye = _linear(act, [(moe_w_out[jf], 0)], _epi_plain, D, f32, tm=tm_e, tn=1024,
                         name="moe_out", group=groups)

        last = i == depth - 1
        if is_moe:
            x2 = _moe_combine_deepnorm(x2, ye, pos, wts, gate, ln_g[i, 1], ln_b[i, 1], S, alpha)
            if not last:
                shift, scale, gate = mod(i + 1, 0)
                h = _modulate(x2, scale, shift, S)
        elif last:
            (x2,) = _deepnorm(x2, y, gate, ln_g[i, 1], ln_b[i, 1], S, alpha)
        else:
            shift, scale, gate_next = mod(i + 1, 0)
            x2, h = _deepnorm(x2, y, gate, ln_g[i, 1], ln_b[i, 1], S, alpha, nxt=(scale, shift))
            gate = gate_next
    return x2.reshape(B, S, D)
```
